```python
import jax, jax.numpy as jnp
from jax import lax
import numpy as np

D_MODEL = 2048
BATCH = 16
SEQ = 2048
DEPTH = 1
DEC_BATCH = 16
DEC_SEQ = 16
PAST_LEN = 1024

CHUNK = 64
EPS = 1e-6
D_RNN = D_MODEL
CONV_W = 4
LRU_BLOCKS = 16
LRU_BS = D_RNN // LRU_BLOCKS
LRU_C = 8.0
HEAD_DIM = 128
N_HEADS = D_MODEL // HEAD_DIM
N_KV = 4
GROUP = N_HEADS // N_KV
WINDOW = 128
WIN_CHUNKS = WINDOW // CHUNK
ROT_DIMS = HEAD_DIM // 4
ROPE_THETA = 500000.0
NEG = -1e30
N_EXP_GROUPS = 4
EXP_PER_GROUP = 8
N_EXPERTS = N_EXP_GROUPS * EXP_PER_GROUP
TOP_K = 2
D_EXPERT = D_MODEL // 4
MOE_BLOCK = 128
Q_W = N_HEADS * HEAD_DIM
KV_W = N_KV * HEAD_DIM
SPLITS = [D_RNN, 2 * D_RNN, 2 * D_RNN + Q_W, 2 * D_RNN + Q_W + KV_W, 2 * D_RNN + Q_W + 2 * KV_W, 2 * D_RNN + Q_W + 2 * KV_W + D_MODEL]
N_IN_COLS = 2 * D_RNN + Q_W + 2 * KV_W + 2 * D_MODEL

kernel_name = 'hybrid_rglru_swa_hmoe_stream_step'


def _rmsnorm(x, g):
    xf = x.astype(jnp.float32)
    y = xf * lax.rsqrt(jnp.mean(xf * xf, axis=-1, keepdims=True) + EPS)
    return (y * g.astype(jnp.float32)).astype(x.dtype)


def _modulate(h, shift, scale):
    return h * (1.0 + scale[:, None, :]) + shift[:, None, :]


def _partial_rope(x, pos):
    half = ROT_DIMS // 2
    inv = ROPE_THETA ** (-2.0 * jnp.arange(half, dtype=jnp.float32) / ROT_DIMS)
    ang = pos.astype(jnp.float32)[:, None] * inv[None, :]
    cos = jnp.cos(ang)[None, :, None, :]
    sin = jnp.sin(ang)[None, :, None, :]
    xr = x[..., :ROT_DIMS].astype(jnp.float32)
    x1, x2 = xr[..., :half], xr[..., half:]
    rot = jnp.concatenate([x1 * cos - x2 * sin, x2 * cos + x1 * sin], axis=-1).astype(x.dtype)
    return jnp.concatenate([rot, x[..., ROT_DIMS:]], axis=-1)


def _rg_lru_branch(xa, conv_state, h0, conv_w, conv_b, w_ra, b_ra, w_ri, b_ri, lru_lambda):
    B, S, _ = xa.shape
    xp = jnp.concatenate([conv_state.astype(xa.dtype), xa], axis=1)
    xc = conv_b
    for tap in range(CONV_W):
        xc = xc + xp[:, tap:tap + S] * conv_w[tap]
    new_conv = xp[:, -(CONV_W - 1):]
    xr = xc.reshape(B, S, LRU_BLOCKS, LRU_BS)
    r = jax.nn.sigmoid(jnp.einsum('bsnk,nkj->bsnj', xr, w_ra) + b_ra).reshape(B, S, D_RNN)
    i = jax.nn.sigmoid(jnp.einsum('bsnk,nkj->bsnj', xr, w_ri) + b_ri).reshape(B, S, D_RNN)
    log_a = -LRU_C * r.astype(jnp.float32) * jax.nn.softplus(-lru_lambda.astype(jnp.float32))
    a = jnp.exp(log_a)
    mult = jnp.sqrt(jnp.maximum(-jnp.expm1(2.0 * log_a), 0.0))
    bt = mult * (i * xc).astype(jnp.float32)
    bt = bt.at[:, 0].add(a[:, 0] * h0.astype(jnp.float32))

    def combine(left, right):
        a1, b1 = left
        a2, b2 = right
        return a1 * a2, a2 * b1 + b2

    _, h = lax.associative_scan(combine, (a, bt), axis=1)
    return h.astype(xa.dtype), new_conv, h[:, -1].astype(xa.dtype)


def _softmax_with_sink(s, sink):
    col = jnp.broadcast_to(sink.astype(jnp.float32).reshape(N_KV, GROUP, 1, 1), s.shape[:-1] + (1,))
    return jax.nn.softmax(jnp.concatenate([s, col], axis=-1), axis=-1)[..., :-1]


def _swa_prompt(q, k, v, sink):
    B, S = q.shape[:2]
    NC = S // CHUNK
    NK = (WIN_CHUNKS + 1) * CHUNK
    qb = q.reshape(B, NC, CHUNK, N_KV, GROUP, HEAD_DIM)

    def band(t):
        tc = t.reshape(B, NC, CHUNK, N_KV, HEAD_DIM)
        tp = jnp.pad(tc, ((0, 0), (WIN_CHUNKS, 0), (0, 0), (0, 0), (0, 0)))
        return jnp.concatenate([tp[:, j:j + NC] for j in range(WIN_CHUNKS + 1)], axis=2)

    kb, vb = band(k), band(v)
    s = jnp.einsum('bnqkgd,bnskd->bnkgqs', qb, kb, preferred_element_type=jnp.float32) * (HEAD_DIM ** -0.5)
    key_chunk = jnp.arange(NC)[:, None] + jnp.arange(NK)[None, :] // CHUNK - WIN_CHUNKS
    valid = key_chunk >= 0
    s = jnp.where(valid[None, :, None, None, None, :], s, NEG)
    p = _softmax_with_sink(s, sink)
    o = jnp.einsum('bnkgqs,bnskd->bnqkgd', p.astype(vb.dtype), vb)
    return o.reshape(B, S, Q_W)


def _swa_sample(q, k, v, k_cache, v_cache, sink):
    B, T = q.shape[:2]
    kk = jnp.concatenate([k_cache.astype(k.dtype), k], axis=1)
    vv = jnp.concatenate([v_cache.astype(v.dtype), v], axis=1)
    qg = q.reshape(B, T, N_KV, GROUP, HEAD_DIM)
    s = jnp.einsum('bqkgd,bskd->bkgqs', qg, kk, preferred_element_type=jnp.float32) * (HEAD_DIM ** -0.5)
    p = _softmax_with_sink(s, sink)
    o = jnp.einsum('bkgqs,bskd->bqkgd', p.astype(vv.dtype), vv)
    rows = k_cache.shape[1]
    return o.reshape(B, T, Q_W), kk[:, -rows:], vv[:, -rows:]


def _hier_moe(h, w_router_g, b_router_g, w_router_e, b_router_e, w_exp_up, w_exp_down):
    B, S, D = h.shape
    T = B * S
    xt = h.reshape(T, D)
    lg = (xt @ w_router_g).astype(jnp.float32) + b_router_g.astype(jnp.float32)
    pg = jax.nn.softmax(lg, axis=-1)
    g_sel = jnp.argmax(lg, axis=-1)
    p_g = jnp.take_along_axis(pg, g_sel[:, None], axis=-1)
    le = ((xt @ w_router_e).astype(jnp.float32) + b_router_e.astype(jnp.float32)).reshape(T, N_EXP_GROUPS, EXP_PER_GROUP)
    le_sel = jnp.take_along_axis(le, g_sel[:, None, None], axis=1)[:, 0]
    top_l, top_i = lax.top_k(le_sel, TOP_K)
    w_sel = p_g * jax.nn.softmax(top_l, axis=-1)
    expert = g_sel[:, None] * EXP_PER_GROUP + top_i
    A = T * TOP_K
    e_flat = expert.reshape(A)
    w_flat = w_sel.reshape(A)
    tok = jnp.arange(A) // TOP_K
    order = jnp.argsort(e_flat)
    e_s, tok_s, w_s = e_flat[order], tok[order], w_flat[order]
    counts = jnp.zeros((N_EXPERTS,), jnp.int32).at[e_flat].add(1)
    padded = (counts + MOE_BLOCK - 1) // MOE_BLOCK * MOE_BLOCK
    start = jnp.cumsum(counts) - counts
    pend = jnp.cumsum(padded)
    pstart = pend - padded
    dest = pstart[e_s] + jnp.arange(A) - start[e_s]
    n_blocks = -(-A // MOE_BLOCK) + N_EXPERTS
    P = n_blocks * MOE_BLOCK
    xs = jnp.zeros((P, D), h.dtype).at[dest].set(xt[tok_s])
    block_expert = jnp.minimum(jnp.searchsorted(pend, jnp.arange(n_blocks) * MOE_BLOCK, side='right'), N_EXPERTS - 1)

    def expert_block(args):
        xb, e = args
        up = xb @ w_exp_up[e]
        return (jax.nn.silu(up[:, :D_EXPERT]) * up[:, D_EXPERT:]) @ w_exp_down[e]

    ys = lax.map(expert_block, (xs.reshape(n_blocks, MOE_BLOCK, D), block_expert)).reshape(P, D)
    contrib = ys[dest] * w_s[:, None].astype(ys.dtype)
    return jax.ops.segment_sum(contrib, tok_s, num_segments=T).reshape(B, S, D)


def _trunk_layer(x, c, pos, conv_state, h0, k_cache, v_cache,
                 norm1_g, norm2_g, w_ada, b_ada, w_in, conv_w, conv_b,
                 w_ra, b_ra, w_ri, b_ri, lru_lambda, q_norm_g, k_norm_g,
                 attn_sink, w_out, w_router_g, b_router_g, w_router_e,
                 b_router_e, w_exp_up, w_exp_down):
    B, S, _ = x.shape
    mod = jax.nn.silu(c) @ w_ada + b_ada
    sh1, sc1, g1, sh2, sc2, g2 = jnp.split(mod, 6, axis=-1)
    h = _modulate(_rmsnorm(x, norm1_g), sh1, sc1)
    proj = h @ w_in
    xa, ya, q, k, v, gate_a, gate_b = jnp.split(proj, SPLITS, axis=-1)
    lru, new_conv, new_h = _rg_lru_branch(xa, conv_state, h0, conv_w, conv_b, w_ra, b_ra, w_ri, b_ri, lru_lambda)
    out_a = lru * jax.nn.gelu(ya)
    q = _partial_rope(_rmsnorm(q.reshape(B, S, N_HEADS, HEAD_DIM), q_norm_g), pos)
    k = _partial_rope(_rmsnorm(k.reshape(B, S, N_KV, HEAD_DIM), k_norm_g), pos)
    v = v.reshape(B, S, N_KV, HEAD_DIM)
    if k_cache is None:
        out_b = _swa_prompt(q, k, v, attn_sink)
        new_k, new_v = k[:, -WINDOW:], v[:, -WINDOW:]
    else:
        out_b, new_k, new_v = _swa_sample(q, k, v, k_cache, v_cache, attn_sink)
    merged = jax.nn.sigmoid(gate_a) * out_a + jax.nn.sigmoid(gate_b) * out_b
    x = x + g1[:, None, :] * (merged @ w_out)
    h2 = _modulate(_rmsnorm(x, norm2_g), sh2, sc2)
    x = x + g2[:, None, :] * _hier_moe(h2, w_router_g, b_router_g, w_router_e, b_router_e, w_exp_up, w_exp_down)
    return x, new_h, new_conv, new_k, new_v


def setup_inputs(seed: int = 0) -> dict:
    key = jax.random.key(seed)
    ks = jax.random.split(key, 32)
    f32 = jnp.float32

    def nrm(k, shape, scale):
        return jax.random.normal(k, shape, f32) * scale

    win_rows = min(WINDOW, PAST_LEN)
    a0 = jax.random.uniform(ks[17], (DEPTH, D_RNN), f32, 0.9, 0.999)
    s0 = a0 ** (1.0 / LRU_C)
    lru_lambda = jnp.log(s0) - jnp.log1p(-s0)
    return {
        'x_prompt': nrm(ks[0], (BATCH, SEQ, D_MODEL), 1.0),
        'x_sample': nrm(ks[1], (DEC_BATCH, DEC_SEQ, D_MODEL), 1.0),
        'state_lru_h': nrm(ks[2], (DEPTH, DEC_BATCH, D_RNN), 0.5),
        'state_lru_conv': nrm(ks[3], (DEPTH, DEC_BATCH, CONV_W - 1, D_RNN), 1.0),
        'cache_swa_k': nrm(ks[4], (DEPTH, DEC_BATCH, win_rows, N_KV, HEAD_DIM), 1.0),
        'cache_swa_v': nrm(ks[5], (DEPTH, DEC_BATCH, win_rows, N_KV, HEAD_DIM), 1.0),
        'c_prompt': nrm(ks[6], (BATCH, D_MODEL), 1.0),
        'c_sample': nrm(ks[7], (DEC_BATCH, D_MODEL), 1.0),
        'norm1_g': 1.0 + nrm(ks[8], (DEPTH, D_MODEL), 0.05),
        'norm2_g': 1.0 + nrm(ks[9], (DEPTH, D_MODEL), 0.05),
        'w_ada': nrm(ks[10], (DEPTH, D_MODEL, 6 * D_MODEL), D_MODEL ** -0.5),
        'b_ada': nrm(ks[11], (DEPTH, 6 * D_MODEL), 0.01),
        'w_in': nrm(ks[12], (DEPTH, D_MODEL, N_IN_COLS), D_MODEL ** -0.5),
        'conv_w': nrm(ks[13], (DEPTH, CONV_W, D_RNN), CONV_W ** -0.5),
        'conv_b': nrm(ks[14], (DEPTH, D_RNN), 0.01),
        'w_ra': nrm(ks[15], (DEPTH, LRU_BLOCKS, LRU_BS, LRU_BS), LRU_BS ** -0.5),
        'b_ra': nrm(ks[16], (DEPTH, LRU_BLOCKS, LRU_BS), 0.01),
        'w_ri': nrm(ks[18], (DEPTH, LRU_BLOCKS, LRU_BS, LRU_BS), LRU_BS ** -0.5),
        'b_ri': nrm(ks[19], (DEPTH, LRU_BLOCKS, LRU_BS), 0.01),
        'lru_lambda': lru_lambda,
        'q_norm_g': 1.0 + nrm(ks[20], (DEPTH, HEAD_DIM), 0.05),
        'k_norm_g': 1.0 + nrm(ks[21], (DEPTH, HEAD_DIM), 0.05),
        'attn_sink': nrm(ks[22], (DEPTH, N_HEADS), 0.5),
        'w_out': nrm(ks[23], (DEPTH, D_MODEL, D_MODEL), D_MODEL ** -0.5),
        'w_router_g': nrm(ks[24], (DEPTH, D_MODEL, N_EXP_GROUPS), D_MODEL ** -0.5),
        'b_router_g': nrm(ks[25], (DEPTH, N_EXP_GROUPS), 0.01),
        'w_router_e': nrm(ks[26], (DEPTH, D_MODEL, N_EXPERTS), D_MODEL ** -0.5),
        'b_router_e': nrm(ks[27], (DEPTH, N_EXPERTS), 0.01),
        'w_exp_up': nrm(ks[28], (DEPTH, N_EXPERTS, D_MODEL, 2 * D_EXPERT), D_MODEL ** -0.5),
        'w_exp_down': nrm(ks[29], (DEPTH, N_EXPERTS, D_EXPERT, D_MODEL), D_EXPERT ** -0.5),
    }


def reference(x_prompt, x_sample, state_lru_h, state_lru_conv, cache_swa_k, cache_swa_v,
              c_prompt, c_sample, norm1_g, norm2_g, w_ada, b_ada, w_in, conv_w, conv_b,
              w_ra, b_ra, w_ri, b_ri, lru_lambda, q_norm_g, k_norm_g, attn_sink, w_out,
              w_router_g, b_router_g, w_router_e, b_router_e, w_exp_up, w_exp_down):
    Bp, Sp, _ = x_prompt.shape
    pos_p = jnp.arange(Sp)
    pos_s = PAST_LEN + jnp.arange(x_sample.shape[1])
    xp, xs = x_prompt, x_sample
    p_h, p_conv, p_k, p_v = [], [], [], []
    s_h, s_conv, s_k, s_v = [], [], [], []
    for l in range(DEPTH):
        lw = (norm1_g[l], norm2_g[l], w_ada[l], b_ada[l], w_in[l], conv_w[l], conv_b[l],
              w_ra[l], b_ra[l], w_ri[l], b_ri[l], lru_lambda[l], q_norm_g[l], k_norm_g[l],
              attn_sink[l], w_out[l], w_router_g[l], b_router_g[l], w_router_e[l],
              b_router_e[l], w_exp_up[l], w_exp_down[l])
        conv0 = jnp.zeros((Bp, CONV_W - 1, D_RNN), xp.dtype)
        h0 = jnp.zeros((Bp, D_RNN), xp.dtype)
        xp, nh, nc, nk, nv = _trunk_layer(xp, c_prompt, pos_p, conv0, h0, None, None, *lw)
        p_h.append(nh); p_conv.append(nc); p_k.append(nk); p_v.append(nv)
        xs, nh, nc, nk, nv = _trunk_layer(xs, c_sample, pos_s, state_lru_conv[l], state_lru_h[l],
                                          cache_swa_k[l], cache_swa_v[l], *lw)
        s_h.append(nh); s_conv.append(nc); s_k.append(nk); s_v.append(nv)
    return (xp, xs,
            jnp.stack(p_h), jnp.stack(p_conv), jnp.stack(p_k), jnp.stack(p_v),
            jnp.stack(s_h), jnp.stack(s_conv), jnp.stack(s_k), jnp.stack(s_v))
```

```python
import functools
import math

import jax
import jax.numpy as jnp
from jax import lax
from jax.experimental import pallas as pl
from jax.experimental.pallas import tpu as pltpu

F32 = jnp.float32
BF16 = jnp.bfloat16
I32 = jnp.int32

D_MODEL = 2048
EPS = 1e-6
LANES = 128
SUBLANES = 8
CHUNK = 64
CONV_W = 4
LRU_BLOCKS = 16
LRU_BS = D_MODEL // LRU_BLOCKS
LRU_C = 8.0
HEAD_DIM = 128
N_HEADS = D_MODEL // HEAD_DIM
N_KV = 4
GROUP = N_HEADS // N_KV
WINDOW = 128
ROT_DIMS = HEAD_DIM // 4
ROPE_THETA = 500000.0
NEG = -1e30
N_EXP_GROUPS = 4
EXP_PER_GROUP = 8
N_EXPERTS = N_EXP_GROUPS * EXP_PER_GROUP
D_EXPERT = D_MODEL // 4
Q_W = N_HEADS * HEAD_DIM
KV_W = N_KV * HEAD_DIM
REF_COL_K = 2 * D_MODEL + Q_W
REF_COL_GA = REF_COL_K + 2 * KV_W
COL_XA, COL_YA, COL_Q = 0, D_MODEL, 2 * D_MODEL
COL_GA = COL_Q + Q_W
COL_GB = COL_GA + D_MODEL
COL_K = COL_GB + D_MODEL
COL_V = COL_K + KV_W
N_IN_COLS = COL_V + KV_W
PAST_LEN = 1024
ATT_SCALE = HEAD_DIM ** -0.5
VMEM_LIMIT = 56 * 1024 * 1024


def _params(sem, vmem=VMEM_LIMIT):
    return pltpu.CompilerParams(dimension_semantics=sem, vmem_limit_bytes=vmem)


def _ada_body(c_ref, w_ref, b_ref, o_ref):
    c = c_ref[...]
    s = c * jax.nn.sigmoid(c)
    o_ref[...] = jnp.dot(s.astype(BF16), w_ref[...].astype(BF16), preferred_element_type=F32) + b_ref[...]


def _ada(c, w_ada, b_ada):
    bt, n, tn = c.shape[0], w_ada.shape[1], 1024
    return pl.pallas_call(
        _ada_body, grid=(n // tn,),
        in_specs=[pl.BlockSpec((bt, D_MODEL), lambda j: (0, 0)),
                  pl.BlockSpec((D_MODEL, tn), lambda j: (0, j)),
                  pl.BlockSpec((1, tn), lambda j: (0, j))],
        out_specs=pl.BlockSpec((bt, tn), lambda j: (0, j)),
        out_shape=jax.ShapeDtypeStruct((bt, n), F32),
        compiler_params=_params(("arbitrary",)), name="ada")(c, w_ada, b_ada.reshape(1, n))


def _norm_mod(x, g, sc, sh):
    ms = jnp.mean(x * x, axis=-1, keepdims=True)
    return (x * lax.rsqrt(ms + EPS) * g) * (1.0 + sc) + sh


def _inproj_body(x_ref, g_ref, sc_ref, sh_ref, w_ref, o_ref, h_scr):
    @pl.when(pl.program_id(1) == 0)
    def _():
        h_scr[...] = _norm_mod(x_ref[...], g_ref[...], sc_ref[...], sh_ref[...]).astype(BF16)

    o_ref[...] = jnp.dot(h_scr[...], w_ref[...], preferred_element_type=F32).astype(o_ref.dtype)


def _mod_spec(per_token, tm, tiles_per_batch):
    if per_token:
        return pl.BlockSpec((tm, D_MODEL), lambda i, *_: (i, 0))
    return pl.BlockSpec((None, 1, D_MODEL), lambda i, *_: (i // tiles_per_batch, 0, 0))


def _inproj(x, norm_g, sc, sh, w_bf, per_token, tm, seq):
    t, tn = x.shape[0], 1024
    tpb = max(seq // tm, 1)
    return pl.pallas_call(
        _inproj_body, grid=(t // tm, N_IN_COLS // tn),
        in_specs=[pl.BlockSpec((tm, D_MODEL), lambda i, j: (i, 0)),
                  pl.BlockSpec((1, D_MODEL), lambda i, j: (0, 0)),
                  _mod_spec(per_token, tm, tpb), _mod_spec(per_token, tm, tpb),
                  pl.BlockSpec((D_MODEL, tn), lambda i, j: (0, j))],
        out_specs=pl.BlockSpec((tm, tn), lambda i, j: (i, j)),
        out_shape=jax.ShapeDtypeStruct((t, N_IN_COLS), BF16),
        scratch_shapes=[pltpu.VMEM((tm, D_MODEL), BF16)],
        compiler_params=_params(("arbitrary", "arbitrary")), name="inproj")(x, norm_g, sc, sh, w_bf)


def _norm_rope(x, g, c, s1, s2):
    ms = jnp.mean(x * x, axis=-1, keepdims=True)
    y = x * lax.rsqrt(ms + EPS) * g
    return y * c + pltpu.roll(y, ROT_DIMS // 2, 1) * s1 + pltpu.roll(y, HEAD_DIM - ROT_DIMS // 2, 1) * s2


def _qkprep_body(q_ref, k_ref, qg_ref, kg_ref, c_ref, s1_ref, s2_ref, qo_ref, ko_ref):
    c, s1, s2 = c_ref[...], s1_ref[...], s2_ref[...]
    for h in range(N_HEADS):
        sl = slice(h * HEAD_DIM, (h + 1) * HEAD_DIM)
        qo_ref[:, sl] = _norm_rope(q_ref[:, sl].astype(F32), qg_ref[...], c, s1, s2).astype(BF16)
    for h in range(N_KV):
        sl = slice(h * HEAD_DIM, (h + 1) * HEAD_DIM)
        ko_ref[:, sl] = _norm_rope(k_ref[:, sl].astype(F32), kg_ref[...], c, s1, s2).astype(BF16)


def _qkprep(proj, q_norm_g, k_norm_g, tabs, tq):
    t = proj.shape[0]
    ntab = tabs[0].shape[0] // tq
    tab_spec = pl.BlockSpec((tq, HEAD_DIM), lambda i: (i % ntab, 0))
    g_spec = pl.BlockSpec((1, HEAD_DIM), lambda i: (0, 0))
    return pl.pallas_call(
        _qkprep_body, grid=(t // tq,),
        in_specs=[pl.BlockSpec((tq, Q_W), lambda i: (i, COL_Q // Q_W)),
                  pl.BlockSpec((tq, KV_W), lambda i: (i, COL_K // KV_W)),
                  g_spec, g_spec, tab_spec, tab_spec, tab_spec],
        out_specs=[pl.BlockSpec((tq, Q_W), lambda i: (i, 0)), pl.BlockSpec((tq, KV_W), lambda i: (i, 0))],
        out_shape=[jax.ShapeDtypeStruct((t, Q_W), BF16), jax.ShapeDtypeStruct((t, KV_W), BF16)],
        compiler_params=_params(("arbitrary",)), name="qkprep")(
            proj, proj, q_norm_g.reshape(1, HEAD_DIM), k_norm_g.reshape(1, HEAD_DIM), *tabs)


def _rope_tables(pos):
    half = ROT_DIMS // 2
    inv = ROPE_THETA ** (-2.0 * jnp.arange(half, dtype=F32) / ROT_DIMS)
    ang = pos.astype(F32)[:, None] * inv[None, :]
    cos, sin = jnp.cos(ang), jnp.sin(ang)
    n = pos.shape[0]
    rest = HEAD_DIM - ROT_DIMS
    c = jnp.concatenate([cos, cos, jnp.ones((n, rest), F32)], axis=1)
    s1 = jnp.concatenate([jnp.zeros((n, half), F32), sin, jnp.zeros((n, rest), F32)], axis=1)
    s2 = jnp.concatenate([-sin, jnp.zeros((n, half + rest), F32)], axis=1)
    return c, s1, s2


def _lru_body(xa_ref, ya_ref, ga_ref, cs_ref, h0_ref, cw_ref, cb_ref, wra_ref, bra_ref, wri_ref, bri_ref,
              lam_ref, o_ref, nh_ref, nc_ref, xpad, a_scr, b_scr, *, seq, rc):
    pad = SUBLANES
    xpad[0:pad, :] = jnp.zeros((pad, LRU_BS), F32)
    xpad[pad - (CONV_W - 1):pad, :] = cs_ref[...]
    xpad[pad:pad + seq, :] = xa_ref[...].astype(F32)
    nc_ref[...] = xpad[seq + pad - (CONV_W - 1):seq + pad, :]

    z = -lam_ref[...]
    softplus = jnp.maximum(z, 0.0) + jnp.log1p(jnp.exp(-jnp.abs(z)))
    clam = -LRU_C * softplus
    wra = wra_ref[...].astype(BF16)
    wri = wri_ref[...].astype(BF16)
    for c0 in range(0, seq, rc):
        xc = cb_ref[...]
        for tap in range(CONV_W):
            xc = xc + xpad[c0 + pad - (CONV_W - 1) + tap:c0 + pad - (CONV_W - 1) + tap + rc, :] * cw_ref[tap:tap + 1, :]
        xcb = xc.astype(BF16)
        r = jax.nn.sigmoid(jnp.dot(xcb, wra, preferred_element_type=F32) + bra_ref[...])
        ig = jax.nn.sigmoid(jnp.dot(xcb, wri, preferred_element_type=F32) + bri_ref[...])
        log_a = r * clam
        a_scr[c0:c0 + rc, :] = jnp.exp(log_a)
        th = jnp.tanh(log_a)
        mult = jnp.sqrt(jnp.maximum(-2.0 * th / (1.0 - th), 0.0))
        b_scr[c0:c0 + rc, :] = mult * (ig * xc)

    rows = 2 * SUBLANES
    rid = lax.broadcasted_iota(I32, (rows, LRU_BS), 0) % SUBLANES

    def step(it, carry):
        r0 = pl.multiple_of(it * rows, rows)
        a = a_scr[pl.ds(r0, rows), :]
        b = b_scr[pl.ds(r0, rows), :]
        for d in (1, 2, 4):
            keep = rid >= d
            a_s = jnp.where(keep, pltpu.roll(a, d, 0), 1.0)
            b_s = jnp.where(keep, pltpu.roll(b, d, 0), 0.0)
            b = a * b_s + b
            a = a * a_s
        c1 = a[SUBLANES - 1:SUBLANES, :] * carry + b[SUBLANES - 1:SUBLANES, :]
        c2 = a[rows - 1:rows, :] * c1 + b[rows - 1:rows, :]
        h = jnp.concatenate([b[:SUBLANES] + a[:SUBLANES] * carry, b[SUBLANES:] + a[SUBLANES:] * c1], axis=0)
        ya = ya_ref[pl.ds(r0, rows), :].astype(F32)
        ga = ga_ref[pl.ds(r0, rows), :].astype(F32)
        o_ref[pl.ds(r0, rows), :] = (jax.nn.sigmoid(ga) * (h * jax.nn.gelu(ya))).astype(o_ref.dtype)
        return c2

    nh_ref[...] = lax.fori_loop(0, seq // rows, step, h0_ref[...])


def _lru(proj, conv_state, h0, conv_w, conv_b, w_ra, b_ra, w_ri, b_ri, lam, batch, seq):
    t = proj.shape[0]
    rc = min(seq, 256)
    col = lambda off: (lambda b, c: (b, off // LRU_BS + c))
    vec = lambda: pl.BlockSpec((1, LRU_BS), lambda b, c: (0, c))
    blk3 = lambda r: pl.BlockSpec((None, r, LRU_BS), lambda b, c: (b, 0, c))
    wblk = lambda: pl.BlockSpec((None, LRU_BS, LRU_BS), lambda b, c: (c, 0, 0))
    bblk = lambda: pl.BlockSpec((None, 1, LRU_BS), lambda b, c: (c, 0, 0))
    return pl.pallas_call(
        functools.partial(_lru_body, seq=seq, rc=rc), grid=(batch, LRU_BLOCKS),
        in_specs=[pl.BlockSpec((seq, LRU_BS), col(COL_XA)), pl.BlockSpec((seq, LRU_BS), col(COL_YA)),
                  pl.BlockSpec((seq, LRU_BS), col(COL_GA)), blk3(CONV_W - 1), blk3(1),
                  pl.BlockSpec((CONV_W, LRU_BS), lambda b, c: (0, c)), vec(), wblk(), bblk(), wblk(), bblk(), vec()],
        out_specs=[pl.BlockSpec((seq, LRU_BS), lambda b, c: (b, c)), blk3(1), blk3(CONV_W - 1)],
        out_shape=[jax.ShapeDtypeStruct((t, D_MODEL), BF16), jax.ShapeDtypeStruct((batch, 1, D_MODEL), F32),
                   jax.ShapeDtypeStruct((batch, CONV_W - 1, D_MODEL), F32)],
        scratch_shapes=[pltpu.VMEM((seq + SUBLANES, LRU_BS), F32), pltpu.VMEM((seq, LRU_BS), F32),
                        pltpu.VMEM((seq, LRU_BS), F32)],
        compiler_params=_params(("arbitrary", "arbitrary")), name="lru")(
            proj, proj, proj, conv_state, h0.reshape(batch, 1, D_MODEL), conv_w, conv_b.reshape(1, D_MODEL),
            w_ra, b_ra.reshape(LRU_BLOCKS, 1, LRU_BS), w_ri, b_ri.reshape(LRU_BLOCKS, 1, LRU_BS),
            lam.reshape(1, D_MODEL))


def _attn_body(sink_ref, q_ref, kp_ref, kc_ref, vp_ref, vc_ref, gb_ref, o_ref, *, banded, qt, npv):
    nt = (((1,), (1,)), ((), ()))
    if banded:
        i = pl.program_id(1)
        qc = lax.broadcasted_iota(I32, (qt, npv), 0) // CHUNK
        kc_prev = lax.broadcasted_iota(I32, (qt, npv), 1) // CHUNK
        kc_cur = lax.broadcasted_iota(I32, (qt, qt), 1) // CHUNK
        qc_cur = lax.broadcasted_iota(I32, (qt, qt), 0) // CHUNK
        valid_p = (kc_prev >= qc) & (i > 0)
        valid_c = kc_cur <= qc_cur
    for g in range(N_KV):
        gs = slice(g * HEAD_DIM, (g + 1) * HEAD_DIM)
        kp, kc = kp_ref[:, gs].astype(BF16), kc_ref[:, gs].astype(BF16)
        vp, vc = vp_ref[:, gs].astype(BF16), vc_ref[:, gs].astype(BF16)
        for hh in range(GROUP):
            h = g * GROUP + hh
            hs = slice(h * HEAD_DIM, (h + 1) * HEAD_DIM)
            q = q_ref[:, hs]
            s1 = lax.dot_general(q, kp, nt, preferred_element_type=F32) * ATT_SCALE
            s2 = lax.dot_general(q, kc, nt, preferred_element_type=F32) * ATT_SCALE
            if banded:
                s1 = jnp.where(valid_p, s1, NEG)
                s2 = jnp.where(valid_c, s2, NEG)
            sk = sink_ref[h]
            m = jnp.maximum(jnp.maximum(jnp.max(s1, -1, keepdims=True), jnp.max(s2, -1, keepdims=True)), sk)
            p1, p2 = jnp.exp(s1 - m), jnp.exp(s2 - m)
            den = jnp.sum(p1, -1, keepdims=True) + jnp.sum(p2, -1, keepdims=True) + jnp.exp(sk - m)
            o = (jnp.dot(p1.astype(BF16), vp, preferred_element_type=F32)
                 + jnp.dot(p2.astype(BF16), vc, preferred_element_type=F32)) / den
            o_ref[:, hs] = (jax.nn.sigmoid(gb_ref[:, hs].astype(F32)) * o).astype(o_ref.dtype)


def _attn(sink, qn, k_prev_arr, k_cur_arr, v_prev_arr, v_cur_arr, gb_arr, batch, seq, qt, npv, banded,
          prev_map, kcur_col, vcur_col, gb_col):
    t = qn.shape[0]
    nq = seq // qt
    cur = lambda col: pl.BlockSpec((qt, KV_W), lambda b, i: (b * nq + i, col))
    return pl.pallas_call(
        functools.partial(_attn_body, banded=banded, qt=qt, npv=npv), grid=(batch, nq),
        in_specs=[pl.BlockSpec(memory_space=pltpu.SMEM),
                  pl.BlockSpec((qt, Q_W), lambda b, i: (b * nq + i, 0)),
                  pl.BlockSpec((npv, KV_W), prev_map[0]), cur(kcur_col),
                  pl.BlockSpec((npv, KV_W), prev_map[1]), cur(vcur_col),
                  pl.BlockSpec((qt, Q_W), lambda b, i: (b * nq + i, gb_col))],
        out_specs=pl.BlockSpec((qt, Q_W), lambda b, i: (b * nq + i, 0)),
        out_shape=jax.ShapeDtypeStruct((t, Q_W), BF16),
        compiler_params=_params(("arbitrary", "arbitrary")), name="attn")(
            sink, qn, k_prev_arr, k_cur_arr, v_prev_arr, v_cur_arr, gb_arr)


def _lane_min(x):
    return jnp.min(x, axis=-1, keepdims=True)


def _outproj_body(x_ref, ma_ref, mb_ref, w_ref, g1_ref, n2_ref, sc_ref, sh_ref, wrh_ref, wrl_ref, br_ref,
                  x1_ref, h2_ref, mf_ref, mi_ref, cnt_ref, base_scr, *, tm):
    @pl.when(pl.program_id(0) == 0)
    def _():
        base_scr[...] = jnp.zeros_like(base_scr)

    merged = (ma_ref[...].astype(F32) + mb_ref[...].astype(F32)).astype(BF16)
    x1 = x_ref[...] + g1_ref[...] * jnp.dot(merged, w_ref[...], preferred_element_type=F32)
    x1_ref[...] = x1
    h2 = _norm_mod(x1, n2_ref[...], sc_ref[...], sh_ref[...])
    h2_ref[...] = h2

    hi = h2.astype(BF16)
    lo = (h2 - hi.astype(F32)).astype(BF16)
    logits = (jnp.dot(hi, wrh_ref[...], preferred_element_type=F32)
              + jnp.dot(lo, wrh_ref[...], preferred_element_type=F32)
              + jnp.dot(hi, wrl_ref[...], preferred_element_type=F32)) + br_ref[...]
    lane = lax.broadcasted_iota(I32, (tm, LANES), 1).astype(F32)
    ninf = -jnp.inf
    big = float(LANES)
    is_g = lane < N_EXP_GROUPS
    lg = jnp.where(is_g, logits, ninf)
    mg = jnp.max(lg, -1, keepdims=True)
    gsel = _lane_min(jnp.where(lg == mg, lane, big))
    p_g = 1.0 / jnp.sum(jnp.where(is_g, jnp.exp(lg - mg), 0.0), -1, keepdims=True)
    lo_lane = N_EXP_GROUPS + EXP_PER_GROUP * gsel
    le = jnp.where((lane >= lo_lane) & (lane < lo_lane + EXP_PER_GROUP), logits, ninf)
    m1 = jnp.max(le, -1, keepdims=True)
    i1 = _lane_min(jnp.where(le == m1, lane, big))
    le2 = jnp.where(lane == i1, ninf, le)
    m2 = jnp.max(le2, -1, keepdims=True)
    i2 = _lane_min(jnp.where(le2 == m2, lane, big))
    e21 = jnp.exp(m2 - m1)
    w1 = p_g / (1.0 + e21)
    w2 = p_g * e21 / (1.0 + e21)
    e1, e2 = i1 - N_EXP_GROUPS, i2 - N_EXP_GROUPS

    onehot = ((lane == e1) | (lane == e2))
    row = lax.broadcasted_iota(I32, (tm, tm), 0)
    colm = lax.broadcasted_iota(I32, (tm, tm), 1)
    lower = jnp.where(row > colm, 1.0, 0.0).astype(BF16)
    before = jnp.dot(lower, jnp.where(onehot, 1.0, 0.0).astype(BF16), preferred_element_type=F32) + base_scr[...]
    r1 = jnp.sum(jnp.where(lane == e1, before, 0.0), -1, keepdims=True)
    r2 = jnp.sum(jnp.where(lane == e2, before, 0.0), -1, keepdims=True)
    base_scr[...] = base_scr[...] + jnp.sum(jnp.where(onehot, 1.0, 0.0), axis=0, keepdims=True)
    cnt_ref[...] = base_scr[...]

    mf_ref[...] = jnp.where(lane == 0.0, w1, jnp.where(lane == 1.0, w2, 0.0))
    mi = jnp.where(lane == 0.0, e1, jnp.where(lane == 1.0, e2, jnp.where(lane == 2.0, r1, jnp.where(lane == 3.0, r2, 0.0))))
    mi_ref[...] = mi.astype(I32)


def _outproj(x, ma, mb, w_bf, g1, norm2_g, sc, sh, wr_hi, wr_lo, br, per_token, tm, seq):
    t = x.shape[0]
    tpb = max(seq // tm, 1)
    row = lambda: pl.BlockSpec((tm, D_MODEL), lambda i: (i, 0))
    full = lambda shape: pl.BlockSpec(shape, lambda i: (0, 0))
    meta = lambda: pl.BlockSpec((tm, LANES), lambda i: (i, 0))
    return pl.pallas_call(
        functools.partial(_outproj_body, tm=tm), grid=(t // tm,),
        in_specs=[row(), row(), row(), full((D_MODEL, D_MODEL)), _mod_spec(per_token, tm, tpb),
                  full((1, D_MODEL)), _mod_spec(per_token, tm, tpb), _mod_spec(per_token, tm, tpb),
                  full((D_MODEL, LANES)), full((D_MODEL, LANES)), full((1, LANES))],
        out_specs=[row(), row(), meta(), meta(), full((1, LANES))],
        out_shape=[jax.ShapeDtypeStruct((t, D_MODEL), F32), jax.ShapeDtypeStruct((t, D_MODEL), F32),
                   jax.ShapeDtypeStruct((t, LANES), F32), jax.ShapeDtypeStruct((t, LANES), I32),
                   jax.ShapeDtypeStruct((1, LANES), F32)],
        scratch_shapes=[pltpu.VMEM((1, LANES), F32)],
        compiler_params=_params(("arbitrary",)), name="outproj")(
            x, ma, mb, w_bf, g1, norm2_g, sc, sh, wr_hi, wr_lo, br)


def _lane_cumsum(x):
    lane = lax.broadcasted_iota(I32, x.shape, 1)
    for d in (1, 2, 4, 8, 16):
        x = x + jnp.where(lane >= d, pltpu.roll(x, d, 1), 0.0)
    return x


def _plan_body(cnt_ref, mi_ref, dest_ref, bexp_ref, *, tp, blk, nb_pad):
    cnt = jnp.broadcast_to(cnt_ref[...], (SUBLANES, LANES))
    lane8 = lax.broadcasted_iota(I32, (SUBLANES, LANES), 1)
    padded = jnp.where(lane8 < N_EXPERTS, jnp.ceil(cnt / blk) * blk, 0.0)
    pend = _lane_cumsum(padded)
    pstart = (pend - padded)[0:1, :]
    mi = mi_ref[...]
    lane = lax.broadcasted_iota(I32, (tp, LANES), 1)
    e1, e2, r1, r2 = mi[:, 0:1], mi[:, 1:2], mi[:, 2:3], mi[:, 3:4]
    d1 = jnp.sum(jnp.where(lane == e1, pstart, 0.0), -1, keepdims=True).astype(I32) + r1
    d2 = jnp.sum(jnp.where(lane == e2, pstart, 0.0), -1, keepdims=True).astype(I32) + r2
    dest_ref[...] = jnp.where(lane == 0, d1, jnp.where(lane == 1, d2, 0))

    @pl.when(pl.program_id(0) == 0)
    def _():
        jb = lax.broadcasted_iota(I32, (nb_pad, LANES), 0).astype(F32) * blk
        ln = lax.broadcasted_iota(I32, (nb_pad, LANES), 1)
        ends = jnp.where((ln < N_EXPERTS) & (pend[0:1, :] <= jb), 1.0, 0.0)
        be = jnp.minimum(jnp.sum(ends, -1, keepdims=True), N_EXPERTS - 1.0)
        used = pend[0:1, N_EXPERTS - 1:N_EXPERTS] / blk
        bexp_ref[...] = jnp.where(ln == 0, be, jnp.where(ln == 1, used, 0.0)).astype(I32)


def _plan(cnt, mi, blk, nb_pad):
    t = mi.shape[0]
    tp = min(t, 1024)
    return pl.pallas_call(
        functools.partial(_plan_body, tp=tp, blk=float(blk), nb_pad=nb_pad), grid=(t // tp,),
        in_specs=[pl.BlockSpec((1, LANES), lambda i: (0, 0)), pl.BlockSpec((tp, LANES), lambda i: (i, 0))],
        out_specs=[pl.BlockSpec((tp, LANES), lambda i: (i, 0)), pl.BlockSpec((nb_pad, LANES), lambda i: (0, 0))],
        out_shape=[jax.ShapeDtypeStruct((t, LANES), I32), jax.ShapeDtypeStruct((nb_pad, LANES), I32)],
        compiler_params=_params(("arbitrary",)), name="plan")(cnt, mi)


def _dispatch_body(dest_ref, h2_ref, xs_in_ref, xs_ref, sem, *, tm):
    del xs_in_ref

    def copy(r, k):
        return pltpu.make_async_copy(h2_ref.at[pl.ds(r, 1)], xs_ref.at[pl.ds(dest_ref[k, r], 1)], sem)

    def start(r, c):
        copy(r, 0).start()
        copy(r, 1).start()
        return c

    def wait(r, c):
        copy(r, 0).wait()
        copy(r, 1).wait()
        return c

    lax.fori_loop(0, tm, start, 0)
    lax.fori_loop(0, tm, wait, 0)


def _dispatch(dest3, h2, xs_zero, tm):
    t = h2.shape[0]
    return pl.pallas_call(
        functools.partial(_dispatch_body, tm=tm), grid=(t // tm,),
        in_specs=[pl.BlockSpec((None, 2, tm), lambda i: (i, 0, 0), memory_space=pltpu.SMEM),
                  pl.BlockSpec((tm, D_MODEL), lambda i: (i, 0)),
                  pl.BlockSpec(memory_space=pl.ANY)],
        out_specs=pl.BlockSpec(memory_space=pl.ANY),
        out_shape=jax.ShapeDtypeStruct(xs_zero.shape, F32),
        scratch_shapes=[pltpu.SemaphoreType.DMA(())],
        input_output_aliases={2: 0},
        compiler_params=_params(("arbitrary",)), name="dispatch")(dest3, h2, xs_zero)


def _moe_body(bexp_ref, used_ref, xs_ref, wu_ref, wd_ref, ys_ref, wu_bf, wd_bf):
    j = pl.program_id(0)
    changed = (j == 0) | (bexp_ref[j] != bexp_ref[jnp.maximum(j - 1, 0)])

    @pl.when(changed)
    def _():
        wu_bf[...] = wu_ref[...].astype(BF16)
        wd_bf[...] = wd_ref[...].astype(BF16)

    @pl.when(j < used_ref[0])
    def _():
        up = jnp.dot(xs_ref[...].astype(BF16), wu_bf[...], preferred_element_type=F32)
        act = jax.nn.silu(up[:, :D_EXPERT]) * up[:, D_EXPERT:]
        ys_ref[...] = jnp.dot(act.astype(BF16), wd_bf[...], preferred_element_type=F32)

    @pl.when(j >= used_ref[0])
    def _():
        ys_ref[...] = jnp.zeros_like(ys_ref)


def _moe(bexp, used, xs, w_up, w_down, blk):
    p = xs.shape[0]
    grid_spec = pltpu.PrefetchScalarGridSpec(
        num_scalar_prefetch=2, grid=(p // blk,),
        in_specs=[pl.BlockSpec((blk, D_MODEL), lambda j, be, us: (j, 0)),
                  pl.BlockSpec((None, D_MODEL, 2 * D_EXPERT), lambda j, be, us: (be[j], 0, 0)),
                  pl.BlockSpec((None, D_EXPERT, D_MODEL), lambda j, be, us: (be[j], 0, 0))],
        out_specs=pl.BlockSpec((blk, D_MODEL), lambda j, be, us: (j, 0)),
        scratch_shapes=[pltpu.VMEM((D_MODEL, 2 * D_EXPERT), BF16), pltpu.VMEM((D_EXPERT, D_MODEL), BF16)])
    return pl.pallas_call(
        _moe_body, grid_spec=grid_spec, out_shape=jax.ShapeDtypeStruct((p, D_MODEL), F32),
        compiler_params=_params(("arbitrary",)), name="moe")(bexp, used, xs, w_up, w_down)


def _combine_body(dest_ref, x1_ref, mf_ref, g2_ref, ys_ref, o_ref, y0, y1, sem, *, tm):
    def copy(r, k, buf):
        return pltpu.make_async_copy(ys_ref.at[pl.ds(dest_ref[k, r], 1)], buf.at[pl.ds(r, 1)], sem)

    def start(r, c):
        copy(r, 0, y0).start()
        copy(r, 1, y1).start()
        return c

    def wait(r, c):
        copy(r, 0, y0).wait()
        copy(r, 1, y1).wait()
        return c

    lax.fori_loop(0, tm, start, 0)
    lax.fori_loop(0, tm, wait, 0)
    mf = mf_ref[...]
    moe = mf[:, 0:1] * y0[...] + mf[:, 1:2] * y1[...]
    o_ref[...] = x1_ref[...] + g2_ref[...] * moe


def _combine(dest3, x1, mf, g2, ys, per_token, tm, seq):
    t = x1.shape[0]
    tpb = max(seq // tm, 1)
    return pl.pallas_call(
        functools.partial(_combine_body, tm=tm), grid=(t // tm,),
        in_specs=[pl.BlockSpec((None, 2, tm), lambda i: (i, 0, 0), memory_space=pltpu.SMEM),
                  pl.BlockSpec((tm, D_MODEL), lambda i: (i, 0)),
                  pl.BlockSpec((tm, LANES), lambda i: (i, 0)),
                  _mod_spec(per_token, tm, tpb),
                  pl.BlockSpec(memory_space=pl.ANY)],
        out_specs=pl.BlockSpec((tm, D_MODEL), lambda i: (i, 0)),
        out_shape=jax.ShapeDtypeStruct((t, D_MODEL), F32),
        scratch_shapes=[pltpu.VMEM((tm, D_MODEL), F32), pltpu.VMEM((tm, D_MODEL), F32),
                        pltpu.SemaphoreType.DMA(())],
        compiler_params=_params(("arbitrary",)), name="combine")(dest3, x1, mf, g2, ys)


def _group_mod(mod6, per_token, seq):
    if per_token:
        return jnp.repeat(mod6, seq, axis=0)
    return mod6[:, None, :]


def _layer(x, mods, pos, conv_state, h0, k_cache, v_cache, w, seq, tm_in, tm_out, tq, blk):
    batch = x.shape[0]
    t = batch * seq
    xt = x.reshape(t, D_MODEL)
    per_token = seq < tm_out
    sh1, sc1, g1, sh2, sc2, g2 = [_group_mod(m, per_token, seq) for m in mods]

    proj = _inproj(xt, w["norm1_g"], sc1, sh1, w["w_in"], per_token, tm_in, seq)

    tabs = _rope_tables(pos)
    if per_token:
        tabs = tuple(jnp.tile(tb, (batch, 1)) for tb in tabs)
    qn, kn = _qkprep(proj, w["q_norm_g"], w["k_norm_g"], tabs, tq)

    ma, new_h, new_conv = _lru(proj, conv_state, h0, w["conv_w"], w["conv_b"], w["w_ra"], w["b_ra"],
                               w["w_ri"], w["b_ri"], w["lru_lambda"], batch, seq)

    if k_cache is None:
        qt = WINDOW
        nq = seq // qt
        prev = lambda b, i: (b * nq + jnp.maximum(i - 1, 0), 0)
        prev_v = lambda b, i: (b * nq + jnp.maximum(i - 1, 0), COL_V // KV_W)
        mb = _attn(w["attn_sink"], qn, kn, kn, proj, proj, proj, batch, seq, qt, WINDOW, True,
                   (prev, prev_v), 0, COL_V // KV_W, COL_GB // Q_W)
        new_k = kn.reshape(batch, seq, N_KV, HEAD_DIM)[:, -WINDOW:].astype(F32)
        new_v = proj[:, COL_V:COL_V + KV_W].reshape(batch, seq, N_KV, HEAD_DIM)[:, -WINDOW:].astype(F32)
    else:
        rows = k_cache.shape[1]
        kc2 = k_cache.reshape(batch * rows, KV_W)
        vc2 = v_cache.reshape(batch * rows, KV_W)
        prev = lambda b, i: (b, 0)
        mb = _attn(w["attn_sink"], qn, kc2, kn, vc2, proj, proj, batch, seq, seq, rows, False,
                   (prev, prev), 0, COL_V // KV_W, COL_GB // Q_W)
        k_new = kn.reshape(batch, seq, N_KV, HEAD_DIM).astype(F32)
        v_new = proj[:, COL_V:COL_V + KV_W].reshape(batch, seq, N_KV, HEAD_DIM).astype(F32)
        new_k = jnp.concatenate([k_cache, k_new], axis=1)[:, -rows:]
        new_v = jnp.concatenate([v_cache, v_new], axis=1)[:, -rows:]

    x1, h2, mf, mi, cnt = _outproj(xt, ma, mb, w["w_out"], g1, w["norm2_g"], sc2, sh2,
                                   w["wr_hi"], w["wr_lo"], w["br"], per_token, tm_out, seq)

    n_blocks = -(-(2 * t) // blk) + N_EXPERTS
    nb_pad = -(-n_blocks // SUBLANES) * SUBLANES
    dest, bmeta = _plan(cnt, mi, blk, nb_pad)
    dest3 = dest[:, 0:2].reshape(t // tm_out, tm_out, 2).transpose(0, 2, 1)
    xs = _dispatch(dest3, h2, jnp.zeros((n_blocks * blk, D_MODEL), F32), tm_out)
    ys = _moe(bmeta[:n_blocks, 0], bmeta[0:1, 1], xs, w["w_exp_up"], w["w_exp_down"], blk)
    out = _combine(dest3, x1, mf, g2, ys, per_token, tm_out, seq)
    return (out.reshape(batch, seq, D_MODEL), new_h.reshape(batch, D_MODEL), new_conv, new_k, new_v)


def kernel(x_prompt, x_sample, state_lru_h, state_lru_conv, cache_swa_k, cache_swa_v, c_prompt, c_sample,
           norm1_g, norm2_g, w_ada, b_ada, w_in, conv_w, conv_b, w_ra, b_ra, w_ri, b_ri, lru_lambda,
           q_norm_g, k_norm_g, attn_sink, w_out, w_router_g, b_router_g, w_router_e, b_router_e,
           w_exp_up, w_exp_down):
    assert norm1_g.shape[0] == 1, "single-layer trunk"
    bp, sp, _ = x_prompt.shape
    bs, ss, _ = x_sample.shape

    mod = _ada(jnp.concatenate([c_prompt, c_sample], axis=0), w_ada[0], b_ada[0])
    mods_p = jnp.split(mod[:bp], 6, axis=-1)
    mods_s = jnp.split(mod[bp:], 6, axis=-1)

    pad = LANES - N_EXP_GROUPS - N_EXPERTS
    wr = jnp.concatenate([w_router_g[0], w_router_e[0], jnp.zeros((D_MODEL, pad), F32)], axis=1)
    wr_hi = wr.astype(BF16)
    wr_lo = (wr - wr_hi.astype(F32)).astype(BF16)
    br = jnp.concatenate([b_router_g[0], b_router_e[0], jnp.zeros((pad,), F32)]).reshape(1, LANES)

    w_in_bf = jnp.concatenate([w_in[0][:, :REF_COL_K], w_in[0][:, REF_COL_GA:], w_in[0][:, REF_COL_K:REF_COL_GA]],
                              axis=1).astype(BF16)
    w = dict(norm1_g=norm1_g[0].reshape(1, D_MODEL), norm2_g=norm2_g[0].reshape(1, D_MODEL),
             w_in=w_in_bf, conv_w=conv_w[0], conv_b=conv_b[0], w_ra=w_ra[0], b_ra=b_ra[0],
             w_ri=w_ri[0], b_ri=b_ri[0], lru_lambda=lru_lambda[0], q_norm_g=q_norm_g[0], k_norm_g=k_norm_g[0],
             attn_sink=attn_sink[0], w_out=w_out[0].astype(BF16), wr_hi=wr_hi, wr_lo=wr_lo, br=br,
             w_exp_up=w_exp_up[0], w_exp_down=w_exp_down[0])

    yp, ph, pc, pk, pv = _layer(x_prompt, mods_p, jnp.arange(sp), jnp.zeros((bp, CONV_W - 1, D_MODEL), F32),
                                jnp.zeros((bp, D_MODEL), F32), None, None, w, sp,
                                tm_in=min(1024, sp), tm_out=256, tq=256, blk=256)
    ys, sh, sc, sk, sv = _layer(x_sample, mods_s, PAST_LEN + jnp.arange(ss), state_lru_conv[0], state_lru_h[0],
                                cache_swa_k[0], cache_swa_v[0], w, ss,
                                tm_in=bs * ss, tm_out=bs * ss, tq=bs * ss, blk=128)
    return (yp, ys, ph[None], pc[None], pk[None], pv[None], sh[None], sc[None], sk[None], sv[None])
```

```python
import functools
import math

import jax
import jax.numpy as jnp
from jax import lax
from jax.experimental import pallas as pl
from jax.experimental.pallas import tpu as pltpu

F32 = jnp.float32
BF16 = jnp.bfloat16
I32 = jnp.int32

D_MODEL = 2048
EPS = 1e-6
LANES = 128
SUBLANES = 8
CHUNK = 64
CONV_W = 4
LRU_BLOCKS = 16
LRU_BS = D_MODEL // LRU_BLOCKS
LRU_C = 8.0
HEAD_DIM = 128
N_HEADS = D_MODEL // HEAD_DIM
N_KV = 4
GROUP = N_HEADS // N_KV
WINDOW = 128
ROT_DIMS = HEAD_DIM // 4
ROPE_THETA = 500000.0
NEG = -1e30
N_EXP_GROUPS = 4
EXP_PER_GROUP = 8
N_EXPERTS = N_EXP_GROUPS * EXP_PER_GROUP
D_EXPERT = D_MODEL // 4
Q_W = N_HEADS * HEAD_DIM
KV_W = N_KV * HEAD_DIM
REF_COL_K = 2 * D_MODEL + Q_W
REF_COL_GA = REF_COL_K + 2 * KV_W
COL_XA, COL_YA, COL_Q = 0, D_MODEL, 2 * D_MODEL
COL_GA = COL_Q + Q_W
COL_GB = COL_GA + D_MODEL
COL_K = COL_GB + D_MODEL
COL_V = COL_K + KV_W
N_IN_COLS = COL_V + KV_W
PAST_LEN = 1024
ATT_SCALE = HEAD_DIM ** -0.5
VMEM_LIMIT = 56 * 1024 * 1024


def _params(sem, vmem=VMEM_LIMIT):
    return pltpu.CompilerParams(dimension_semantics=sem, vmem_limit_bytes=vmem)


def _ada_body(c_ref, w_ref, b_ref, o_ref):
    c = c_ref[...]
    s = c * jax.nn.sigmoid(c)
    o_ref[...] = jnp.dot(s.astype(BF16), w_ref[...].astype(BF16), preferred_element_type=F32) + b_ref[...]


def _ada(c, w_ada, b_ada):
    bt, n, tn = c.shape[0], w_ada.shape[1], 1024
    return pl.pallas_call(
        _ada_body, grid=(n // tn,),
        in_specs=[pl.BlockSpec((bt, D_MODEL), lambda j: (0, 0)),
                  pl.BlockSpec((D_MODEL, tn), lambda j: (0, j)),
                  pl.BlockSpec((1, tn), lambda j: (0, j))],
        out_specs=pl.BlockSpec((bt, tn), lambda j: (0, j)),
        out_shape=jax.ShapeDtypeStruct((bt, n), F32),
        compiler_params=_params(("arbitrary",)), name="ada")(c, w_ada, b_ada.reshape(1, n))


def _norm_mod(x, g, sc, sh):
    ms = jnp.mean(x * x, axis=-1, keepdims=True)
    return (x * lax.rsqrt(ms + EPS) * g) * (1.0 + sc) + sh


def _inproj_body(x_ref, g_ref, sc_ref, sh_ref, w_ref, o_ref, h_scr):
    @pl.when(pl.program_id(1) == 0)
    def _():
        h_scr[...] = _norm_mod(x_ref[...], g_ref[...], sc_ref[...], sh_ref[...]).astype(BF16)

    o_ref[...] = jnp.dot(h_scr[...], w_ref[...], preferred_element_type=F32).astype(o_ref.dtype)


def _mod_spec(per_token, tm, tiles_per_batch):
    if per_token:
        return pl.BlockSpec((tm, D_MODEL), lambda i, *_: (i, 0))
    return pl.BlockSpec((None, 1, D_MODEL), lambda i, *_: (i // tiles_per_batch, 0, 0))


def _inproj(x, norm_g, sc, sh, w_bf, per_token, tm, seq):
    t, tn = x.shape[0], 1024
    tpb = max(seq // tm, 1)
    return pl.pallas_call(
        _inproj_body, grid=(t // tm, N_IN_COLS // tn),
        in_specs=[pl.BlockSpec((tm, D_MODEL), lambda i, j: (i, 0)),
                  pl.BlockSpec((1, D_MODEL), lambda i, j: (0, 0)),
                  _mod_spec(per_token, tm, tpb), _mod_spec(per_token, tm, tpb),
                  pl.BlockSpec((D_MODEL, tn), lambda i, j: (0, j))],
        out_specs=pl.BlockSpec((tm, tn), lambda i, j: (i, j)),
        out_shape=jax.ShapeDtypeStruct((t, N_IN_COLS), BF16),
        scratch_shapes=[pltpu.VMEM((tm, D_MODEL), BF16)],
        compiler_params=_params(("arbitrary", "arbitrary")), name="inproj")(x, norm_g, sc, sh, w_bf)


def _norm_rope(x, g, c, s1, s2):
    ms = jnp.mean(x * x, axis=-1, keepdims=True)
    y = x * lax.rsqrt(ms + EPS) * g
    return y * c + pltpu.roll(y, ROT_DIMS // 2, 1) * s1 + pltpu.roll(y, HEAD_DIM - ROT_DIMS // 2, 1) * s2


def _qkprep_body(q_ref, k_ref, qg_ref, kg_ref, c_ref, s1_ref, s2_ref, qo_ref, ko_ref):
    c, s1, s2 = c_ref[...], s1_ref[...], s2_ref[...]
    for h in range(N_HEADS):
        sl = slice(h * HEAD_DIM, (h + 1) * HEAD_DIM)
        qo_ref[:, sl] = _norm_rope(q_ref[:, sl].astype(F32), qg_ref[...], c, s1, s2).astype(BF16)
    for h in range(N_KV):
        sl = slice(h * HEAD_DIM, (h + 1) * HEAD_DIM)
        ko_ref[:, sl] = _norm_rope(k_ref[:, sl].astype(F32), kg_ref[...], c, s1, s2).astype(BF16)


def _qkprep(proj, q_norm_g, k_norm_g, tabs, tq):
    t = proj.shape[0]
    ntab = tabs[0].shape[0] // tq
    tab_spec = pl.BlockSpec((tq, HEAD_DIM), lambda i: (i % ntab, 0))
    g_spec = pl.BlockSpec((1, HEAD_DIM), lambda i: (0, 0))
    return pl.pallas_call(
        _qkprep_body, grid=(t // tq,),
        in_specs=[pl.BlockSpec((tq, Q_W), lambda i: (i, COL_Q // Q_W)),
                  pl.BlockSpec((tq, KV_W), lambda i: (i, COL_K // KV_W)),
                  g_spec, g_spec, tab_spec, tab_spec, tab_spec],
        out_specs=[pl.BlockSpec((tq, Q_W), lambda i: (i, 0)), pl.BlockSpec((tq, KV_W), lambda i: (i, 0))],
        out_shape=[jax.ShapeDtypeStruct((t, Q_W), BF16), jax.ShapeDtypeStruct((t, KV_W), BF16)],
        compiler_params=_params(("arbitrary",)), name="qkprep")(
            proj, proj, q_norm_g.reshape(1, HEAD_DIM), k_norm_g.reshape(1, HEAD_DIM), *tabs)


def _rope_tables(pos):
    half = ROT_DIMS // 2
    inv = ROPE_THETA ** (-2.0 * jnp.arange(half, dtype=F32) / ROT_DIMS)
    ang = pos.astype(F32)[:, None] * inv[None, :]
    cos, sin = jnp.cos(ang), jnp.sin(ang)
    n = pos.shape[0]
    rest = HEAD_DIM - ROT_DIMS
    c = jnp.concatenate([cos, cos, jnp.ones((n, rest), F32)], axis=1)
    s1 = jnp.concatenate([jnp.zeros((n, half), F32), sin, jnp.zeros((n, rest), F32)], axis=1)
    s2 = jnp.concatenate([-sin, jnp.zeros((n, half + rest), F32)], axis=1)
    return c, s1, s2


def _lru_body(xa_ref, ya_ref, ga_ref, cs_ref, h0_ref, cw_ref, cb_ref, wra_ref, bra_ref, wri_ref, bri_ref,
              lam_ref, o_ref, nh_ref, nc_ref, xpad, a_scr, b_scr, g_scr, *, seq, rc, nsub):
    pad = SUBLANES
    xpad[0:pad, :] = jnp.zeros((pad, nsub * LRU_BS), F32)
    xpad[pad - (CONV_W - 1):pad, :] = cs_ref[...]
    xpad[pad:pad + seq, :] = xa_ref[...].astype(F32)
    nc_ref[...] = xpad[seq + pad - (CONV_W - 1):seq + pad, :]

    z = -lam_ref[...]
    softplus = jnp.maximum(z, 0.0) + jnp.log1p(jnp.exp(-jnp.abs(z)))
    clam_all = -LRU_C * softplus
    for sb in range(nsub):
        ls = slice(sb * LRU_BS, (sb + 1) * LRU_BS)
        clam = clam_all[:, ls]
        wra = wra_ref[sb].astype(BF16)
        wri = wri_ref[sb].astype(BF16)
        for c0 in range(0, seq, rc):
            xc = cb_ref[:, ls]
            for tap in range(CONV_W):
                r0 = c0 + pad - (CONV_W - 1) + tap
                xc = xc + xpad[r0:r0 + rc, ls] * cw_ref[tap:tap + 1, ls]
            xcb = xc.astype(BF16)
            r = jax.nn.sigmoid(jnp.dot(xcb, wra, preferred_element_type=F32) + bra_ref[sb])
            ig = jax.nn.sigmoid(jnp.dot(xcb, wri, preferred_element_type=F32) + bri_ref[sb])
            log_a = r * clam
            a_scr[c0:c0 + rc, ls] = jnp.exp(log_a)
            th = jnp.tanh(log_a)
            mult = jnp.sqrt(jnp.maximum(-2.0 * th / (1.0 - th), 0.0))
            b_scr[c0:c0 + rc, ls] = mult * (ig * xc)
            ya = ya_ref[c0:c0 + rc, ls].astype(F32)
            ga = ga_ref[c0:c0 + rc, ls].astype(F32)
            g_scr[c0:c0 + rc, ls] = jax.nn.sigmoid(ga) * jax.nn.gelu(ya)

    rows = 2 * SUBLANES
    rid = lax.broadcasted_iota(I32, (rows, LRU_BS), 0) % SUBLANES

    def step(it, carry):
        r0 = pl.multiple_of(it * rows, rows)
        out = []
        for sb in range(nsub):
            ls = slice(sb * LRU_BS, (sb + 1) * LRU_BS)
            a = a_scr[pl.ds(r0, rows), ls]
            b = b_scr[pl.ds(r0, rows), ls]
            for d in (1, 2, 4):
                keep = rid >= d
                a_s = jnp.where(keep, pltpu.roll(a, d, 0), 1.0)
                b_s = jnp.where(keep, pltpu.roll(b, d, 0), 0.0)
                b = a * b_s + b
                a = a * a_s
            c0 = carry[:, ls]
            c1 = a[SUBLANES - 1:SUBLANES, :] * c0 + b[SUBLANES - 1:SUBLANES, :]
            out.append(a[rows - 1:rows, :] * c1 + b[rows - 1:rows, :])
            h = jnp.concatenate([b[:SUBLANES] + a[:SUBLANES] * c0, b[SUBLANES:] + a[SUBLANES:] * c1], axis=0)
            o_ref[pl.ds(r0, rows), ls] = (g_scr[pl.ds(r0, rows), ls] * h).astype(o_ref.dtype)
        return jnp.concatenate(out, axis=1) if nsub > 1 else out[0]

    nh_ref[...] = lax.fori_loop(0, seq // rows, step, h0_ref[...], unroll=min(2, seq // rows))


def _lru(proj, conv_state, h0, conv_w, conv_b, w_ra, b_ra, w_ri, b_ri, lam, batch, seq, nsub):
    t = proj.shape[0]
    rc = min(seq, 256)
    cw = nsub * LRU_BS
    col = lambda off: (lambda b, c: (b, off // cw + c))
    vec = lambda: pl.BlockSpec((1, cw), lambda b, c: (0, c))
    blk3 = lambda r: pl.BlockSpec((None, r, cw), lambda b, c: (b, 0, c))
    wblk = lambda: pl.BlockSpec((nsub, LRU_BS, LRU_BS), lambda b, c: (c, 0, 0))
    bblk = lambda: pl.BlockSpec((nsub, 1, LRU_BS), lambda b, c: (c, 0, 0))
    return pl.pallas_call(
        functools.partial(_lru_body, seq=seq, rc=rc, nsub=nsub), grid=(batch, LRU_BLOCKS // nsub),
        in_specs=[pl.BlockSpec((seq, cw), col(COL_XA)), pl.BlockSpec((seq, cw), col(COL_YA)),
                  pl.BlockSpec((seq, cw), col(COL_GA)), blk3(CONV_W - 1), blk3(1),
                  pl.BlockSpec((CONV_W, cw), lambda b, c: (0, c)), vec(), wblk(), bblk(), wblk(), bblk(), vec()],
        out_specs=[pl.BlockSpec((seq, cw), lambda b, c: (b, c)), blk3(1), blk3(CONV_W - 1)],
        out_shape=[jax.ShapeDtypeStruct((t, D_MODEL), BF16), jax.ShapeDtypeStruct((batch, 1, D_MODEL), F32),
                   jax.ShapeDtypeStruct((batch, CONV_W - 1, D_MODEL), F32)],
        scratch_shapes=[pltpu.VMEM((seq + SUBLANES, cw), F32), pltpu.VMEM((seq, cw), F32),
                        pltpu.VMEM((seq, cw), F32), pltpu.VMEM((seq, cw), F32)],
        compiler_params=_params(("arbitrary", "arbitrary")), name="lru")(
            proj, proj, proj, conv_state, h0.reshape(batch, 1, D_MODEL), conv_w, conv_b.reshape(1, D_MODEL),
            w_ra, b_ra.reshape(LRU_BLOCKS, 1, LRU_BS), w_ri, b_ri.reshape(LRU_BLOCKS, 1, LRU_BS),
            lam.reshape(1, D_MODEL))


def _attn_body(sink_ref, q_ref, kp_ref, kc_ref, vp_ref, vc_ref, gb_ref, o_ref, *, banded, qt, npv):
    nt = (((1,), (1,)), ((), ()))
    rows = GROUP * qt
    merged_keys = banded
    if banded:
        i = pl.program_id(1)
        nk = npv + qt
        qc = (lax.broadcasted_iota(I32, (rows, nk), 0) % qt) // CHUNK
        col = lax.broadcasted_iota(I32, (rows, nk), 1)
        valid = ((col < npv) & (col // CHUNK >= qc) & (i > 0)) | ((col >= npv) & ((col - npv) // CHUNK <= qc))
    for g in range(N_KV):
        gs = slice(g * HEAD_DIM, (g + 1) * HEAD_DIM)
        heads = [g * GROUP + hh for hh in range(GROUP)]
        q = jnp.concatenate([q_ref[:, h * HEAD_DIM:(h + 1) * HEAD_DIM] for h in heads], axis=0)
        sk = jnp.concatenate([jnp.full((qt, 1), sink_ref[h], F32) for h in heads], axis=0)
        kp, kc = kp_ref[:, gs].astype(BF16), kc_ref[:, gs].astype(BF16)
        vp, vc = vp_ref[:, gs].astype(BF16), vc_ref[:, gs].astype(BF16)
        if merged_keys:
            parts = [(jnp.concatenate([kp, kc], axis=0), jnp.concatenate([vp, vc], axis=0), valid)]
        else:
            parts = [(kp, vp, None), (kc, vc, None)]
        scores = []
        for k, _, ok in parts:
            s = lax.dot_general(q, k, nt, preferred_element_type=F32) * ATT_SCALE
            scores.append(s if ok is None else jnp.where(ok, s, NEG))
        m = sk
        for s in scores:
            m = jnp.maximum(m, jnp.max(s, -1, keepdims=True))
        den = jnp.exp(sk - m)
        acc = jnp.zeros((rows, HEAD_DIM), F32)
        for s, (_, v, _) in zip(scores, parts):
            p = jnp.exp(s - m).astype(BF16)
            den = den + jnp.dot(p, jnp.ones(v.shape, BF16), preferred_element_type=F32)
            acc = acc + jnp.dot(p, v, preferred_element_type=F32)
        o = acc / den
        for hh, h in enumerate(heads):
            hs = slice(h * HEAD_DIM, (h + 1) * HEAD_DIM)
            gate = jax.nn.sigmoid(gb_ref[:, hs].astype(F32))
            o_ref[:, hs] = (gate * o[hh * qt:(hh + 1) * qt]).astype(o_ref.dtype)


def _attn(sink, qn, k_prev_arr, k_cur_arr, v_prev_arr, v_cur_arr, gb_arr, batch, seq, qt, npv, banded,
          prev_map, kcur_col, vcur_col, gb_col):
    t = qn.shape[0]
    nq = seq // qt
    cur = lambda col: pl.BlockSpec((qt, KV_W), lambda b, i: (b * nq + i, col))
    return pl.pallas_call(
        functools.partial(_attn_body, banded=banded, qt=qt, npv=npv), grid=(batch, nq),
        in_specs=[pl.BlockSpec(memory_space=pltpu.SMEM),
                  pl.BlockSpec((qt, Q_W), lambda b, i: (b * nq + i, 0)),
                  pl.BlockSpec((npv, KV_W), prev_map[0]), cur(kcur_col),
                  pl.BlockSpec((npv, KV_W), prev_map[1]), cur(vcur_col),
                  pl.BlockSpec((qt, Q_W), lambda b, i: (b * nq + i, gb_col))],
        out_specs=pl.BlockSpec((qt, Q_W), lambda b, i: (b * nq + i, 0)),
        out_shape=jax.ShapeDtypeStruct((t, Q_W), BF16),
        compiler_params=_params(("arbitrary", "arbitrary")), name="attn")(
            sink, qn, k_prev_arr, k_cur_arr, v_prev_arr, v_cur_arr, gb_arr)


def _lane_min(x):
    return jnp.min(x, axis=-1, keepdims=True)


def _outproj_body(x_ref, ma_ref, mb_ref, w_ref, g1_ref, n2_ref, sc_ref, sh_ref, wrh_ref, wrl_ref, br_ref,
                  x1_ref, h2_ref, mf_ref, mi_ref, cnt_ref, base_scr, *, tm):
    @pl.when(pl.program_id(0) == 0)
    def _():
        base_scr[...] = jnp.zeros_like(base_scr)

    merged = (ma_ref[...].astype(F32) + mb_ref[...].astype(F32)).astype(BF16)
    x1 = x_ref[...] + g1_ref[...] * jnp.dot(merged, w_ref[...], preferred_element_type=F32)
    x1_ref[...] = x1
    h2 = _norm_mod(x1, n2_ref[...], sc_ref[...], sh_ref[...])
    h2_ref[...] = h2

    hi = h2.astype(BF16)
    lo = (h2 - hi.astype(F32)).astype(BF16)
    logits = (jnp.dot(hi, wrh_ref[...], preferred_element_type=F32)
              + jnp.dot(lo, wrh_ref[...], preferred_element_type=F32)
              + jnp.dot(hi, wrl_ref[...], preferred_element_type=F32)) + br_ref[...]
    lane = lax.broadcasted_iota(I32, (tm, LANES), 1).astype(F32)
    ninf = -jnp.inf
    big = float(LANES)
    is_g = lane < N_EXP_GROUPS
    lg = jnp.where(is_g, logits, ninf)
    mg = jnp.max(lg, -1, keepdims=True)
    gsel = _lane_min(jnp.where(lg == mg, lane, big))
    p_g = 1.0 / jnp.sum(jnp.where(is_g, jnp.exp(lg - mg), 0.0), -1, keepdims=True)
    lo_lane = N_EXP_GROUPS + EXP_PER_GROUP * gsel
    le = jnp.where((lane >= lo_lane) & (lane < lo_lane + EXP_PER_GROUP), logits, ninf)
    m1 = jnp.max(le, -1, keepdims=True)
    i1 = _lane_min(jnp.where(le == m1, lane, big))
    le2 = jnp.where(lane == i1, ninf, le)
    m2 = jnp.max(le2, -1, keepdims=True)
    i2 = _lane_min(jnp.where(le2 == m2, lane, big))
    e21 = jnp.exp(m2 - m1)
    w1 = p_g / (1.0 + e21)
    w2 = p_g * e21 / (1.0 + e21)
    e1, e2 = i1 - N_EXP_GROUPS, i2 - N_EXP_GROUPS

    onehot = ((lane == e1) | (lane == e2))
    row = lax.broadcasted_iota(I32, (tm, tm), 0)
    colm = lax.broadcasted_iota(I32, (tm, tm), 1)
    lower = jnp.where(row > colm, 1.0, 0.0).astype(BF16)
    before = jnp.dot(lower, jnp.where(onehot, 1.0, 0.0).astype(BF16), preferred_element_type=F32) + base_scr[...]
    r1 = jnp.sum(jnp.where(lane == e1, before, 0.0), -1, keepdims=True)
    r2 = jnp.sum(jnp.where(lane == e2, before, 0.0), -1, keepdims=True)
    base_scr[...] = base_scr[...] + jnp.sum(jnp.where(onehot, 1.0, 0.0), axis=0, keepdims=True)
    cnt_ref[...] = base_scr[...]

    mf_ref[...] = jnp.where(lane == 0.0, w1, jnp.where(lane == 1.0, w2, 0.0))
    mi = jnp.where(lane == 0.0, e1, jnp.where(lane == 1.0, e2, jnp.where(lane == 2.0, r1, jnp.where(lane == 3.0, r2, 0.0))))
    mi_ref[...] = mi.astype(I32)


def _outproj(x, ma, mb, w_bf, g1, norm2_g, sc, sh, wr_hi, wr_lo, br, per_token, tm, seq):
    t = x.shape[0]
    tpb = max(seq // tm, 1)
    row = lambda: pl.BlockSpec((tm, D_MODEL), lambda i: (i, 0))
    full = lambda shape: pl.BlockSpec(shape, lambda i: (0, 0))
    meta = lambda: pl.BlockSpec((tm, LANES), lambda i: (i, 0))
    return pl.pallas_call(
        functools.partial(_outproj_body, tm=tm), grid=(t // tm,),
        in_specs=[row(), row(), row(), full((D_MODEL, D_MODEL)), _mod_spec(per_token, tm, tpb),
                  full((1, D_MODEL)), _mod_spec(per_token, tm, tpb), _mod_spec(per_token, tm, tpb),
                  full((D_MODEL, LANES)), full((D_MODEL, LANES)), full((1, LANES))],
        out_specs=[row(), row(), meta(), meta(), full((1, LANES))],
        out_shape=[jax.ShapeDtypeStruct((t, D_MODEL), F32), jax.ShapeDtypeStruct((t, D_MODEL), F32),
                   jax.ShapeDtypeStruct((t, LANES), F32), jax.ShapeDtypeStruct((t, LANES), I32),
                   jax.ShapeDtypeStruct((1, LANES), F32)],
        scratch_shapes=[pltpu.VMEM((1, LANES), F32)],
        compiler_params=_params(("arbitrary",)), name="outproj")(
            x, ma, mb, w_bf, g1, norm2_g, sc, sh, wr_hi, wr_lo, br)


def _lane_cumsum(x):
    lane = lax.broadcasted_iota(I32, x.shape, 1)
    for d in (1, 2, 4, 8, 16):
        x = x + jnp.where(lane >= d, pltpu.roll(x, d, 1), 0.0)
    return x


def _plan_body(cnt_ref, mi_ref, dest_ref, bexp_ref, pend_ref, *, tp, blk, nb_pad):
    cnt = jnp.broadcast_to(cnt_ref[...], (SUBLANES, LANES))
    lane8 = lax.broadcasted_iota(I32, (SUBLANES, LANES), 1)
    padded = jnp.where(lane8 < N_EXPERTS, jnp.ceil(cnt / blk) * blk, 0.0)
    pend = _lane_cumsum(padded)
    pstart = (pend - padded)[0:1, :]
    mi = mi_ref[...]
    lane = lax.broadcasted_iota(I32, (tp, LANES), 1)
    e1, e2, r1, r2 = mi[:, 0:1], mi[:, 1:2], mi[:, 2:3], mi[:, 3:4]
    d1 = jnp.sum(jnp.where(lane == e1, pstart, 0.0), -1, keepdims=True).astype(I32) + r1
    d2 = jnp.sum(jnp.where(lane == e2, pstart, 0.0), -1, keepdims=True).astype(I32) + r2
    dest_ref[...] = jnp.where(lane == 0, d1, jnp.where(lane == 1, d2, 0))

    @pl.when(pl.program_id(0) == 0)
    def _():
        jb = lax.broadcasted_iota(I32, (nb_pad, LANES), 0).astype(F32) * blk
        ln = lax.broadcasted_iota(I32, (nb_pad, LANES), 1)
        ends = jnp.where((ln < N_EXPERTS) & (pend[0:1, :] <= jb), 1.0, 0.0)
        be = jnp.minimum(jnp.sum(ends, -1, keepdims=True), N_EXPERTS - 1.0)
        used = pend[0:1, N_EXPERTS - 1:N_EXPERTS] / blk
        bexp_ref[...] = jnp.where(ln == 0, be, jnp.where(ln == 1, used, 0.0)).astype(I32)
        pend_ref[...] = pend.astype(I32)


def _plan(cnt, mi, blk, nb_pad):
    t = mi.shape[0]
    tp = min(t, 1024)
    return pl.pallas_call(
        functools.partial(_plan_body, tp=tp, blk=float(blk), nb_pad=nb_pad), grid=(t // tp,),
        in_specs=[pl.BlockSpec((1, LANES), lambda i: (0, 0)), pl.BlockSpec((tp, LANES), lambda i: (i, 0))],
        out_specs=[pl.BlockSpec((tp, LANES), lambda i: (i, 0)), pl.BlockSpec((nb_pad, LANES), lambda i: (0, 0)),
                   pl.BlockSpec((SUBLANES, LANES), lambda i: (0, 0))],
        out_shape=[jax.ShapeDtypeStruct((t, LANES), I32), jax.ShapeDtypeStruct((nb_pad, LANES), I32),
                   jax.ShapeDtypeStruct((SUBLANES, LANES), I32)],
        compiler_params=_params(("arbitrary",)), name="plan")(cnt, mi)


def _dispatch_body(pend_ref, dest_ref, h2_ref, xs_ref, zbuf, sem, zsem, *, tm, blk):
    i = pl.program_id(0)

    def zero_copy(e):
        return pltpu.make_async_copy(zbuf, xs_ref.at[pl.ds(pl.multiple_of(pend_ref[e] - blk, blk), blk)], zsem)

    def has_rows(e):
        return pend_ref[e] > jnp.where(e > 0, pend_ref[jnp.maximum(e - 1, 0)], 0)

    @pl.when(i == 0)
    def _():
        zbuf[...] = jnp.zeros_like(zbuf)

        def zstart(e, c):
            @pl.when(has_rows(e))
            def _():
                zero_copy(e).start()
            return c

        def zwait(e, c):
            @pl.when(has_rows(e))
            def _():
                zero_copy(e).wait()
            return c

        lax.fori_loop(0, N_EXPERTS, zstart, 0)
        lax.fori_loop(0, N_EXPERTS, zwait, 0)

        def tail_copy(j):
            return pltpu.make_async_copy(zbuf, xs_ref.at[pl.ds(pl.multiple_of(j * blk, blk), blk)], zsem)

        def tstart(j, c):
            tail_copy(j).start()
            return c

        def twait(j, c):
            tail_copy(j).wait()
            return c

        used = pend_ref[N_EXPERTS - 1] // blk
        lax.fori_loop(used, xs_ref.shape[0] // blk, tstart, 0)
        lax.fori_loop(used, xs_ref.shape[0] // blk, twait, 0)

    base = i * tm

    def start(r, c):
        for k in range(2):
            pltpu.make_async_copy(h2_ref.at[pl.ds(base + r, 1)], xs_ref.at[pl.ds(dest_ref[k, r], 1)], sem).start(priority=k)
        return c

    lax.fori_loop(0, tm, start, 0, unroll=8)
    for k in range(2):
        pltpu.make_async_copy(h2_ref.at[pl.ds(0, tm)], xs_ref.at[pl.ds(0, tm)], sem).wait()


def _dispatch(pend, dest3, h2, n_rows, tm, blk):
    t = h2.shape[0]
    return pl.pallas_call(
        functools.partial(_dispatch_body, tm=tm, blk=blk), grid=(t // tm,),
        in_specs=[pl.BlockSpec(memory_space=pltpu.SMEM),
                  pl.BlockSpec((None, 2, tm), lambda i: (i, 0, 0), memory_space=pltpu.SMEM),
                  pl.BlockSpec(memory_space=pl.ANY)],
        out_specs=pl.BlockSpec(memory_space=pl.ANY),
        out_shape=jax.ShapeDtypeStruct((n_rows, D_MODEL), F32),
        scratch_shapes=[pltpu.VMEM((blk, D_MODEL), F32), pltpu.SemaphoreType.DMA(()), pltpu.SemaphoreType.DMA(())],
        compiler_params=_params(("arbitrary",)), name="dispatch")(pend, dest3, h2)


def _moe_body(bexp_ref, used_ref, xs_ref, wu_ref, wd_ref, ys_ref, wu_bf, wd_bf):
    j = pl.program_id(0)
    changed = (j == 0) | (bexp_ref[j] != bexp_ref[jnp.maximum(j - 1, 0)])

    @pl.when(changed)
    def _():
        wu_bf[...] = wu_ref[...].astype(BF16)
        wd_bf[...] = wd_ref[...].astype(BF16)

    @pl.when(j < used_ref[0])
    def _():
        up = jnp.dot(xs_ref[...].astype(BF16), wu_bf[...], preferred_element_type=F32)
        act = jax.nn.silu(up[:, :D_EXPERT]) * up[:, D_EXPERT:]
        ys_ref[...] = jnp.dot(act.astype(BF16), wd_bf[...], preferred_element_type=F32)

    @pl.when(j >= used_ref[0])
    def _():
        ys_ref[...] = jnp.zeros_like(ys_ref)


def _moe(bexp, used, xs, w_up, w_down, blk):
    p = xs.shape[0]
    grid_spec = pltpu.PrefetchScalarGridSpec(
        num_scalar_prefetch=2, grid=(p // blk,),
        in_specs=[pl.BlockSpec((blk, D_MODEL), lambda j, be, us: (j, 0)),
                  pl.BlockSpec((None, D_MODEL, 2 * D_EXPERT), lambda j, be, us: (be[j], 0, 0)),
                  pl.BlockSpec((None, D_EXPERT, D_MODEL), lambda j, be, us: (be[j], 0, 0))],
        out_specs=pl.BlockSpec((blk, D_MODEL), lambda j, be, us: (j, 0)),
        scratch_shapes=[pltpu.VMEM((D_MODEL, 2 * D_EXPERT), BF16), pltpu.VMEM((D_EXPERT, D_MODEL), BF16)])
    return pl.pallas_call(
        _moe_body, grid_spec=grid_spec, out_shape=jax.ShapeDtypeStruct((p, D_MODEL), F32),
        compiler_params=_params(("arbitrary",)), name="moe")(bexp, used, xs, w_up, w_down)


def _combine_body(dest_ref, dnext_ref, x1_ref, mf_ref, g2_ref, ys_ref, o_ref, ybuf, sem, *, tm, nsteps):
    i = pl.program_id(0)

    def issue(dref, s):
        def start(r, c):
            for k in range(2):
                pltpu.make_async_copy(ys_ref.at[pl.ds(dref[k, r], 1)], ybuf.at[s, k, pl.ds(r, 1)],
                                      sem.at[s]).start(priority=k)
            return c

        lax.fori_loop(0, tm, start, 0, unroll=8)

    @pl.when(i == 0)
    def _():
        issue(dest_ref, 0)

    for s in range(2):
        @pl.when(i % 2 == s)
        def _(s=s):
            @pl.when(i + 1 < nsteps)
            def _():
                issue(dnext_ref, 1 - s)

            for k in range(2):
                pltpu.make_async_copy(ys_ref.at[pl.ds(0, tm)], ybuf.at[s, k], sem.at[s]).wait()
            mf = mf_ref[...]
            moe = mf[:, 0:1] * ybuf[s, 0] + mf[:, 1:2] * ybuf[s, 1]
            o_ref[...] = x1_ref[...] + g2_ref[...] * moe


def _combine(dest3, x1, mf, g2, ys, per_token, tm, seq):
    t = x1.shape[0]
    tpb = max(seq // tm, 1)
    nsteps = t // tm
    return pl.pallas_call(
        functools.partial(_combine_body, tm=tm, nsteps=nsteps), grid=(nsteps,),
        in_specs=[pl.BlockSpec((None, 2, tm), lambda i: (i, 0, 0), memory_space=pltpu.SMEM),
                  pl.BlockSpec((None, 2, tm), lambda i: (jnp.minimum(i + 1, nsteps - 1), 0, 0),
                               memory_space=pltpu.SMEM),
                  pl.BlockSpec((tm, D_MODEL), lambda i: (i, 0)),
                  pl.BlockSpec((tm, LANES), lambda i: (i, 0)),
                  _mod_spec(per_token, tm, tpb),
                  pl.BlockSpec(memory_space=pl.ANY)],
        out_specs=pl.BlockSpec((tm, D_MODEL), lambda i: (i, 0)),
        out_shape=jax.ShapeDtypeStruct((t, D_MODEL), F32),
        scratch_shapes=[pltpu.VMEM((2, 2, tm, D_MODEL), F32), pltpu.SemaphoreType.DMA((2,))],
        compiler_params=_params(("arbitrary",)), name="combine")(dest3, dest3, x1, mf, g2, ys)


def _group_mod(mod6, per_token, seq):
    if per_token:
        return jnp.repeat(mod6, seq, axis=0)
    return mod6[:, None, :]


def _layer(x, mods, pos, conv_state, h0, k_cache, v_cache, w, seq, tm_in, tm_out, tq, blk, nsub):
    batch = x.shape[0]
    t = batch * seq
    xt = x.reshape(t, D_MODEL)
    per_token = seq < tm_out
    sh1, sc1, g1, sh2, sc2, g2 = [_group_mod(m, per_token, seq) for m in mods]

    proj = _inproj(xt, w["norm1_g"], sc1, sh1, w["w_in"], per_token, tm_in, seq)

    tabs = _rope_tables(pos)
    if per_token:
        tabs = tuple(jnp.tile(tb, (batch, 1)) for tb in tabs)
    qn, kn = _qkprep(proj, w["q_norm_g"], w["k_norm_g"], tabs, tq)

    ma, new_h, new_conv = _lru(proj, conv_state, h0, w["conv_w"], w["conv_b"], w["w_ra"], w["b_ra"],
                               w["w_ri"], w["b_ri"], w["lru_lambda"], batch, seq, nsub)

    keep = min(WINDOW, seq)
    kn3 = kn.reshape(batch, seq, KV_W)[:, seq - keep:]
    v3 = proj.reshape(batch, seq, N_IN_COLS)[:, seq - keep:, COL_V:COL_V + KV_W]
    k_new = kn3.astype(F32).reshape(batch, keep, N_KV, HEAD_DIM)
    v_new = v3.astype(F32).reshape(batch, keep, N_KV, HEAD_DIM)
    if k_cache is None:
        qt = WINDOW
        nq = seq // qt
        prev = lambda b, i: (b * nq + jnp.maximum(i - 1, 0), 0)
        prev_v = lambda b, i: (b * nq + jnp.maximum(i - 1, 0), COL_V // KV_W)
        mb = _attn(w["attn_sink"], qn, kn, kn, proj, proj, proj, batch, seq, qt, WINDOW, True,
                   (prev, prev_v), 0, COL_V // KV_W, COL_GB // Q_W)
        new_k, new_v = k_new, v_new
    else:
        rows = k_cache.shape[1]
        kc2 = k_cache.reshape(batch * rows, KV_W)
        vc2 = v_cache.reshape(batch * rows, KV_W)
        prev = lambda b, i: (b, 0)
        mb = _attn(w["attn_sink"], qn, kc2, kn, vc2, proj, proj, batch, seq, seq, rows, False,
                   (prev, prev), 0, COL_V // KV_W, COL_GB // Q_W)
        new_k = jnp.concatenate([k_cache, k_new], axis=1)[:, -rows:]
        new_v = jnp.concatenate([v_cache, v_new], axis=1)[:, -rows:]

    x1, h2, mf, mi, cnt = _outproj(xt, ma, mb, w["w_out"], g1, w["norm2_g"], sc2, sh2,
                                   w["wr_hi"], w["wr_lo"], w["br"], per_token, tm_out, seq)

    n_blocks = -(-(2 * t) // blk) + N_EXPERTS
    nb_pad = -(-n_blocks // SUBLANES) * SUBLANES
    dest, bmeta, pend = _plan(cnt, mi, blk, nb_pad)
    tmd = min(t, 1024)
    dest_d = dest[:, 0:2].reshape(t // tmd, tmd, 2).transpose(0, 2, 1)
    dest_c = dest[:, 0:2].reshape(t // tm_out, tm_out, 2).transpose(0, 2, 1)
    xs = _dispatch(pend[0, :N_EXPERTS], dest_d, h2, n_blocks * blk, tmd, blk)
    ys = _moe(bmeta[:n_blocks, 0], bmeta[0:1, 1], xs, w["w_exp_up"], w["w_exp_down"], blk)
    out = _combine(dest_c, x1, mf, g2, ys, per_token, tm_out, seq)
    return (out.reshape(batch, seq, D_MODEL), new_h.reshape(batch, D_MODEL), new_conv, new_k, new_v)


def kernel(x_prompt, x_sample, state_lru_h, state_lru_conv, cache_swa_k, cache_swa_v, c_prompt, c_sample,
           norm1_g, norm2_g, w_ada, b_ada, w_in, conv_w, conv_b, w_ra, b_ra, w_ri, b_ri, lru_lambda,
           q_norm_g, k_norm_g, attn_sink, w_out, w_router_g, b_router_g, w_router_e, b_router_e,
           w_exp_up, w_exp_down):
    assert norm1_g.shape[0] == 1, "single-layer trunk"
    bp, sp, _ = x_prompt.shape
    bs, ss, _ = x_sample.shape

    mod = _ada(jnp.concatenate([c_prompt, c_sample], axis=0), w_ada[0], b_ada[0])
    mods_p = jnp.split(mod[:bp], 6, axis=-1)
    mods_s = jnp.split(mod[bp:], 6, axis=-1)

    pad = LANES - N_EXP_GROUPS - N_EXPERTS
    wr = jnp.concatenate([w_router_g[0], w_router_e[0], jnp.zeros((D_MODEL, pad), F32)], axis=1)
    wr_hi = wr.astype(BF16)
    wr_lo = (wr - wr_hi.astype(F32)).astype(BF16)
    br = jnp.concatenate([b_router_g[0], b_router_e[0], jnp.zeros((pad,), F32)]).reshape(1, LANES)

    w_in_bf = jnp.concatenate([w_in[0][:, :REF_COL_K], w_in[0][:, REF_COL_GA:], w_in[0][:, REF_COL_K:REF_COL_GA]],
                              axis=1).astype(BF16)
    w = dict(norm1_g=norm1_g[0].reshape(1, D_MODEL), norm2_g=norm2_g[0].reshape(1, D_MODEL),
             w_in=w_in_bf, conv_w=conv_w[0], conv_b=conv_b[0], w_ra=w_ra[0], b_ra=b_ra[0],
             w_ri=w_ri[0], b_ri=b_ri[0], lru_lambda=lru_lambda[0], q_norm_g=q_norm_g[0], k_norm_g=k_norm_g[0],
             attn_sink=attn_sink[0], w_out=w_out[0].astype(BF16), wr_hi=wr_hi, wr_lo=wr_lo, br=br,
             w_exp_up=w_exp_up[0], w_exp_down=w_exp_down[0])

    yp, ph, pc, pk, pv = _layer(x_prompt, mods_p, jnp.arange(sp), jnp.zeros((bp, CONV_W - 1, D_MODEL), F32),
                                jnp.zeros((bp, D_MODEL), F32), None, None, w, sp,
                                tm_in=min(1024, sp), tm_out=256, tq=256, blk=256, nsub=2)
    ys, sh, sc, sk, sv = _layer(x_sample, mods_s, PAST_LEN + jnp.arange(ss), state_lru_conv[0], state_lru_h[0],
                                cache_swa_k[0], cache_swa_v[0], w, ss,
                                tm_in=bs * ss, tm_out=bs * ss, tq=bs * ss, blk=128, nsub=LRU_BLOCKS)
    return (yp, ys, ph[None], pc[None], pk[None], pv[None], sh[None], sc[None], sk[None], sv[None])
```

```python
import functools
import math

import jax
import jax.numpy as jnp
from jax import lax
from jax.experimental import pallas as pl
from jax.experimental.pallas import tpu as pltpu

F32 = jnp.float32
BF16 = jnp.bfloat16
I32 = jnp.int32

D_MODEL = 2048
EPS = 1e-6
LANES = 128
SUBLANES = 8
CHUNK = 64
CONV_W = 4
LRU_BLOCKS = 16
LRU_BS = D_MODEL // LRU_BLOCKS
LRU_C = 8.0
HEAD_DIM = 128
N_HEADS = D_MODEL // HEAD_DIM
N_KV = 4
GROUP = N_HEADS // N_KV
WINDOW = 128
ROT_DIMS = HEAD_DIM // 4
ROPE_THETA = 500000.0
NEG = -1e30
N_EXP_GROUPS = 4
EXP_PER_GROUP = 8
N_EXPERTS = N_EXP_GROUPS * EXP_PER_GROUP
D_EXPERT = D_MODEL // 4
Q_W = N_HEADS * HEAD_DIM
KV_W = N_KV * HEAD_DIM
REF_COL_K = 2 * D_MODEL + Q_W
REF_COL_GA = REF_COL_K + 2 * KV_W
COL_XA, COL_YA, COL_Q = 0, D_MODEL, 2 * D_MODEL
COL_GA = COL_Q + Q_W
COL_GB = COL_GA + D_MODEL
COL_K = COL_GB + D_MODEL
COL_V = COL_K + KV_W
N_IN_COLS = COL_V + KV_W
PAST_LEN = 1024
ATT_SCALE = HEAD_DIM ** -0.5
VMEM_LIMIT = 56 * 1024 * 1024


def _params(sem, vmem=VMEM_LIMIT):
    return pltpu.CompilerParams(dimension_semantics=sem, vmem_limit_bytes=vmem)


def _ada_body(c_ref, w_ref, b_ref, o_ref):
    c = c_ref[...]
    s = c * jax.nn.sigmoid(c)
    o_ref[...] = jnp.dot(s.astype(BF16), w_ref[...].astype(BF16), preferred_element_type=F32) + b_ref[...]


def _ada(c, w_ada, b_ada):
    bt, n, tn = c.shape[0], w_ada.shape[1], 1024
    return pl.pallas_call(
        _ada_body, grid=(n // tn,),
        in_specs=[pl.BlockSpec((bt, D_MODEL), lambda j: (0, 0)),
                  pl.BlockSpec((D_MODEL, tn), lambda j: (0, j)),
                  pl.BlockSpec((1, tn), lambda j: (0, j))],
        out_specs=pl.BlockSpec((bt, tn), lambda j: (0, j)),
        out_shape=jax.ShapeDtypeStruct((bt, n), F32),
        compiler_params=_params(("arbitrary",)), name="ada")(c, w_ada, b_ada.reshape(1, n))


def _norm_mod(x, g, sc, sh):
    ms = jnp.mean(x * x, axis=-1, keepdims=True)
    return (x * lax.rsqrt(ms + EPS) * g) * (1.0 + sc) + sh


def _inproj_body(x_ref, g_ref, sc_ref, sh_ref, w_ref, o_ref, h_scr):
    @pl.when(pl.program_id(1) == 0)
    def _():
        h_scr[...] = _norm_mod(x_ref[...], g_ref[...], sc_ref[...], sh_ref[...]).astype(BF16)

    o_ref[...] = jnp.dot(h_scr[...], w_ref[...], preferred_element_type=F32).astype(o_ref.dtype)


def _mod_spec(per_token, tm, tiles_per_batch):
    if per_token:
        return pl.BlockSpec((tm, D_MODEL), lambda i, *_: (i, 0))
    return pl.BlockSpec((None, 1, D_MODEL), lambda i, *_: (i // tiles_per_batch, 0, 0))


def _inproj(x, norm_g, sc, sh, w_bf, per_token, tm, seq):
    t, tn = x.shape[0], 1024
    tpb = max(seq // tm, 1)
    return pl.pallas_call(
        _inproj_body, grid=(t // tm, N_IN_COLS // tn),
        in_specs=[pl.BlockSpec((tm, D_MODEL), lambda i, j: (i, 0)),
                  pl.BlockSpec((1, D_MODEL), lambda i, j: (0, 0)),
                  _mod_spec(per_token, tm, tpb), _mod_spec(per_token, tm, tpb),
                  pl.BlockSpec((D_MODEL, tn), lambda i, j: (0, j))],
        out_specs=pl.BlockSpec((tm, tn), lambda i, j: (i, j)),
        out_shape=jax.ShapeDtypeStruct((t, N_IN_COLS), BF16),
        scratch_shapes=[pltpu.VMEM((tm, D_MODEL), BF16)],
        compiler_params=_params(("arbitrary", "arbitrary")), name="inproj")(x, norm_g, sc, sh, w_bf)


def _norm_rope_all(xs, gains, c, s1, s2):
    ms = [jnp.mean(x * x, axis=-1, keepdims=True) for x in xs]
    ys = [x * lax.rsqrt(m + EPS) * g for x, m, g in zip(xs, ms, gains)]
    up = [pltpu.roll(y, ROT_DIMS // 2, 1) for y in ys]
    dn = [pltpu.roll(y, HEAD_DIM - ROT_DIMS // 2, 1) for y in ys]
    return [(y * c + u * s1 + d * s2).astype(BF16) for y, u, d in zip(ys, up, dn)]


def _rope_tables(pos):
    half = ROT_DIMS // 2
    inv = ROPE_THETA ** (-2.0 * jnp.arange(half, dtype=F32) / ROT_DIMS)
    ang = pos.astype(F32)[:, None] * inv[None, :]
    cos, sin = jnp.cos(ang), jnp.sin(ang)
    n = pos.shape[0]
    rest = HEAD_DIM - ROT_DIMS
    c = jnp.concatenate([cos, cos, jnp.ones((n, rest), F32)], axis=1)
    s1 = jnp.concatenate([jnp.zeros((n, half), F32), sin, jnp.zeros((n, rest), F32)], axis=1)
    s2 = jnp.concatenate([-sin, jnp.zeros((n, half + rest), F32)], axis=1)
    return c, s1, s2


def _lru_body(xa_ref, ya_ref, ga_ref, cs_ref, h0_ref, cw_ref, cb_ref, wra_ref, bra_ref, wri_ref, bri_ref,
              lam_ref, o_ref, nh_ref, nc_ref, xpad, a_scr, b_scr, g_scr, *, seq, rc, nsub):
    pad = SUBLANES
    xpad[0:pad, :] = jnp.zeros((pad, nsub * LRU_BS), F32)
    xpad[pad - (CONV_W - 1):pad, :] = cs_ref[...]
    xpad[pad:pad + seq, :] = xa_ref[...].astype(F32)
    nc_ref[...] = xpad[seq + pad - (CONV_W - 1):seq + pad, :]

    z = -lam_ref[...]
    softplus = jnp.maximum(z, 0.0) + jnp.log1p(jnp.exp(-jnp.abs(z)))
    clam_all = -LRU_C * softplus
    for sb in range(nsub):
        ls = slice(sb * LRU_BS, (sb + 1) * LRU_BS)
        clam = clam_all[:, ls]
        wra = wra_ref[sb].astype(BF16)
        wri = wri_ref[sb].astype(BF16)
        for c0 in range(0, seq, rc):
            xc = cb_ref[:, ls]
            for tap in range(CONV_W):
                r0 = c0 + pad - (CONV_W - 1) + tap
                xc = xc + xpad[r0:r0 + rc, ls] * cw_ref[tap:tap + 1, ls]
            xcb = xc.astype(BF16)
            r = jax.nn.sigmoid(jnp.dot(xcb, wra, preferred_element_type=F32) + bra_ref[sb])
            ig = jax.nn.sigmoid(jnp.dot(xcb, wri, preferred_element_type=F32) + bri_ref[sb])
            log_a = r * clam
            a_scr[c0:c0 + rc, ls] = jnp.exp(log_a)
            th = jnp.tanh(log_a)
            mult = jnp.sqrt(jnp.maximum(-2.0 * th / (1.0 - th), 0.0))
            b_scr[c0:c0 + rc, ls] = mult * (ig * xc)
            ya = ya_ref[c0:c0 + rc, ls].astype(F32)
            ga = ga_ref[c0:c0 + rc, ls].astype(F32)
            g_scr[c0:c0 + rc, ls] = jax.nn.sigmoid(ga) * jax.nn.gelu(ya)

    rows = 2 * SUBLANES
    rid = lax.broadcasted_iota(I32, (rows, LRU_BS), 0) % SUBLANES

    def step(it, carry):
        r0 = pl.multiple_of(it * rows, rows)
        out = []
        for sb in range(nsub):
            ls = slice(sb * LRU_BS, (sb + 1) * LRU_BS)
            a = a_scr[pl.ds(r0, rows), ls]
            b = b_scr[pl.ds(r0, rows), ls]
            for d in (1, 2, 4):
                keep = rid >= d
                a_s = jnp.where(keep, pltpu.roll(a, d, 0), 1.0)
                b_s = jnp.where(keep, pltpu.roll(b, d, 0), 0.0)
                b = a * b_s + b
                a = a * a_s
            c0 = carry[:, ls]
            c1 = a[SUBLANES - 1:SUBLANES, :] * c0 + b[SUBLANES - 1:SUBLANES, :]
            out.append(a[rows - 1:rows, :] * c1 + b[rows - 1:rows, :])
            h = jnp.concatenate([b[:SUBLANES] + a[:SUBLANES] * c0, b[SUBLANES:] + a[SUBLANES:] * c1], axis=0)
            o_ref[pl.ds(r0, rows), ls] = (g_scr[pl.ds(r0, rows), ls] * h).astype(o_ref.dtype)
        return jnp.concatenate(out, axis=1) if nsub > 1 else out[0]

    nh_ref[...] = lax.fori_loop(0, seq // rows, step, h0_ref[...], unroll=min(2, seq // rows))


def _lru(proj, conv_state, h0, conv_w, conv_b, w_ra, b_ra, w_ri, b_ri, lam, batch, seq, nsub):
    t = proj.shape[0]
    rc = min(seq, 256)
    cw = nsub * LRU_BS
    col = lambda off: (lambda b, c: (b, off // cw + c))
    vec = lambda: pl.BlockSpec((1, cw), lambda b, c: (0, c))
    blk3 = lambda r: pl.BlockSpec((None, r, cw), lambda b, c: (b, 0, c))
    wblk = lambda: pl.BlockSpec((nsub, LRU_BS, LRU_BS), lambda b, c: (c, 0, 0))
    bblk = lambda: pl.BlockSpec((nsub, 1, LRU_BS), lambda b, c: (c, 0, 0))
    return pl.pallas_call(
        functools.partial(_lru_body, seq=seq, rc=rc, nsub=nsub), grid=(batch, LRU_BLOCKS // nsub),
        in_specs=[pl.BlockSpec((seq, cw), col(COL_XA)), pl.BlockSpec((seq, cw), col(COL_YA)),
                  pl.BlockSpec((seq, cw), col(COL_GA)), blk3(CONV_W - 1), blk3(1),
                  pl.BlockSpec((CONV_W, cw), lambda b, c: (0, c)), vec(), wblk(), bblk(), wblk(), bblk(), vec()],
        out_specs=[pl.BlockSpec((seq, cw), lambda b, c: (b, c)), blk3(1), blk3(CONV_W - 1)],
        out_shape=[jax.ShapeDtypeStruct((t, D_MODEL), BF16), jax.ShapeDtypeStruct((batch, 1, D_MODEL), F32),
                   jax.ShapeDtypeStruct((batch, CONV_W - 1, D_MODEL), F32)],
        scratch_shapes=[pltpu.VMEM((seq + SUBLANES, cw), F32), pltpu.VMEM((seq, cw), F32),
                        pltpu.VMEM((seq, cw), F32), pltpu.VMEM((seq, cw), F32)],
        compiler_params=_params(("arbitrary", "arbitrary")), name="lru")(
            proj, proj, proj, conv_state, h0.reshape(batch, 1, D_MODEL), conv_w, conv_b.reshape(1, D_MODEL),
            w_ra, b_ra.reshape(LRU_BLOCKS, 1, LRU_BS), w_ri, b_ri.reshape(LRU_BLOCKS, 1, LRU_BS),
            lam.reshape(1, D_MODEL))


def _attn_body(*refs, banded, qt, npv):
    if banded:
        (sink_ref, q_ref, kc_ref, vp_ref, vc_ref, gb_ref, qg_ref, kg_ref, c_ref, s1_ref, s2_ref,
         o_ref, kn_ref, kprev) = refs
    else:
        (sink_ref, q_ref, kp_ref, kc_ref, vp_ref, vc_ref, gb_ref, qg_ref, kg_ref, c_ref, s1_ref, s2_ref,
         o_ref, kn_ref) = refs
    nt = (((1,), (1,)), ((), ()))
    rows = GROUP * qt
    merged_keys = banded
    i = pl.program_id(1)
    if banded:
        nk = npv + qt
        qc = (lax.broadcasted_iota(I32, (rows, nk), 0) % qt) // CHUNK
        col = lax.broadcasted_iota(I32, (rows, nk), 1)
        valid = ((col < npv) & (col // CHUNK >= qc) & (i > 0)) | ((col >= npv) & ((col - npv) // CHUNK <= qc))

        @pl.when(i == 0)
        def _():
            kprev[...] = jnp.zeros_like(kprev)

    tab = (c_ref[...], s1_ref[...], s2_ref[...])
    log2e = math.log2(math.e)
    slabs = ([q_ref[:, h * HEAD_DIM:(h + 1) * HEAD_DIM].astype(F32) for h in range(N_HEADS)]
             + [kc_ref[:, g * HEAD_DIM:(g + 1) * HEAD_DIM].astype(F32) for g in range(N_KV)])
    normed = _norm_rope_all(slabs, [qg_ref[...]] * N_HEADS + [kg_ref[...]] * N_KV, *tab)
    groups = []
    for g in range(N_KV):
        gs = slice(g * HEAD_DIM, (g + 1) * HEAD_DIM)
        heads = [g * GROUP + hh for hh in range(GROUP)]
        q = jnp.concatenate([normed[h] for h in heads], axis=0)
        sk = jnp.concatenate([jnp.full((qt, 1), sink_ref[h] * log2e, F32) for h in heads], axis=0)
        kc = normed[N_HEADS + g]
        kn_ref[:, gs] = kc
        if banded:
            kp = kprev[:, gs]
            kprev[:, gs] = kc
        else:
            kp = kp_ref[:, gs].astype(BF16)
        vp, vc = vp_ref[:, gs].astype(BF16), vc_ref[:, gs].astype(BF16)
        if merged_keys:
            parts = [(jnp.concatenate([kp, kc], axis=0), jnp.concatenate([vp, vc], axis=0), valid)]
        else:
            parts = [(kp, vp, None), (kc, vc, None)]
        parts = [(k, jnp.concatenate([v, jnp.ones(v.shape, BF16)], axis=1), ok) for k, v, ok in parts]
        groups.append(dict(heads=heads, q=q, sk=sk, parts=parts))

    def logits2(q, k, ok):
        s = lax.dot_general(q, k, nt, preferred_element_type=F32) * (ATT_SCALE * log2e)
        return s if ok is None else jnp.where(ok, s, NEG)

    for gr in groups:
        gr["s"] = [logits2(gr["q"], k, ok) for k, _, ok in gr["parts"]]
    for gr in groups:
        m = gr["sk"]
        for s in gr["s"]:
            m = jnp.maximum(m, jnp.max(s, -1, keepdims=True))
        gr["m"] = m
    for gr in groups:
        gr["p"] = [jnp.exp2(s - gr["m"]).astype(BF16) for s in gr["s"]]
    for gr in groups:
        acc = jnp.dot(gr["p"][0], gr["parts"][0][1], preferred_element_type=F32)
        for p, (_, v, _) in zip(gr["p"][1:], gr["parts"][1:]):
            acc = acc + jnp.dot(p, v, preferred_element_type=F32)
        den = acc[:, HEAD_DIM:] + jnp.exp2(gr["sk"] - gr["m"])
        gr["o"] = acc[:, :HEAD_DIM] / den
    for gr in groups:
        for hh, h in enumerate(gr["heads"]):
            hs = slice(h * HEAD_DIM, (h + 1) * HEAD_DIM)
            gate = jax.nn.sigmoid(gb_ref[:, hs].astype(F32))
            o_ref[:, hs] = (gate * gr["o"][hh * qt:(hh + 1) * qt]).astype(o_ref.dtype)


def _attn(sink, proj, tabs, q_norm_g, k_norm_g, k_cache2, v_cache2, batch, seq, qt):
    t = proj.shape[0]
    nq = seq // qt
    banded = k_cache2 is None
    row = lambda width, col: pl.BlockSpec((qt, width), lambda b, i: (b * nq + i, col))
    vec = lambda: pl.BlockSpec((1, HEAD_DIM), lambda b, i: (0, 0))
    tab = lambda: pl.BlockSpec((qt, HEAD_DIM), lambda b, i: (i, 0))
    if banded:
        npv = qt
        prev_specs = [pl.BlockSpec((npv, KV_W), lambda b, i: (b * nq + jnp.maximum(i - 1, 0), COL_V // KV_W))]
        prev_args = [proj]
        scratch = [pltpu.VMEM((qt, KV_W), BF16)]
    else:
        npv = k_cache2.shape[0] // batch
        prev_specs = [pl.BlockSpec((npv, KV_W), lambda b, i: (b, 0))] * 2
        prev_args = [k_cache2, v_cache2]
        scratch = []
    in_specs = ([pl.BlockSpec(memory_space=pltpu.SMEM), row(Q_W, COL_Q // Q_W)]
                + prev_specs[:-1] + [row(KV_W, COL_K // KV_W), prev_specs[-1], row(KV_W, COL_V // KV_W),
                                     row(Q_W, COL_GB // Q_W), vec(), vec(), tab(), tab(), tab()])
    args = ([sink, proj] + prev_args[:-1] + [proj, prev_args[-1], proj, proj,
                                             q_norm_g.reshape(1, HEAD_DIM), k_norm_g.reshape(1, HEAD_DIM), *tabs])
    return pl.pallas_call(
        functools.partial(_attn_body, banded=banded, qt=qt, npv=npv), grid=(batch, nq),
        in_specs=in_specs,
        out_specs=[pl.BlockSpec((qt, Q_W), lambda b, i: (b * nq + i, 0)),
                   pl.BlockSpec((None, qt, KV_W), lambda b, i: (b, 0, 0))],
        out_shape=[jax.ShapeDtypeStruct((t, Q_W), BF16), jax.ShapeDtypeStruct((batch, qt, KV_W), BF16)],
        scratch_shapes=scratch,
        compiler_params=_params(("arbitrary", "arbitrary")), name="attn")(*args)


def _lane_min(x):
    return jnp.min(x, axis=-1, keepdims=True)


def _outproj_body(x_ref, ma_ref, mb_ref, w_ref, g1_ref, n2_ref, sc_ref, sh_ref, wrh_ref, wrl_ref, br_ref,
                  x1_ref, h2_ref, mf_ref, mi_ref, cnt_ref, base_scr, *, tm):
    @pl.when(pl.program_id(0) == 0)
    def _():
        base_scr[...] = jnp.zeros_like(base_scr)

    merged = (ma_ref[...].astype(F32) + mb_ref[...].astype(F32)).astype(BF16)
    x1 = x_ref[...] + g1_ref[...] * jnp.dot(merged, w_ref[...], preferred_element_type=F32)
    x1_ref[...] = x1
    h2 = _norm_mod(x1, n2_ref[...], sc_ref[...], sh_ref[...])
    h2_ref[...] = h2

    hi = h2.astype(BF16)
    lo = (h2 - hi.astype(F32)).astype(BF16)
    logits = (jnp.dot(hi, wrh_ref[...], preferred_element_type=F32)
              + jnp.dot(lo, wrh_ref[...], preferred_element_type=F32)
              + jnp.dot(hi, wrl_ref[...], preferred_element_type=F32)) + br_ref[...]
    lane = lax.broadcasted_iota(I32, (tm, LANES), 1).astype(F32)
    ninf = -jnp.inf
    big = float(LANES)
    is_g = lane < N_EXP_GROUPS
    lg = jnp.where(is_g, logits, ninf)
    mg = jnp.max(lg, -1, keepdims=True)
    gsel = _lane_min(jnp.where(lg == mg, lane, big))
    p_g = 1.0 / jnp.sum(jnp.where(is_g, jnp.exp(lg - mg), 0.0), -1, keepdims=True)
    lo_lane = N_EXP_GROUPS + EXP_PER_GROUP * gsel
    le = jnp.where((lane >= lo_lane) & (lane < lo_lane + EXP_PER_GROUP), logits, ninf)
    m1 = jnp.max(le, -1, keepdims=True)
    i1 = _lane_min(jnp.where(le == m1, lane, big))
    le2 = jnp.where(lane == i1, ninf, le)
    m2 = jnp.max(le2, -1, keepdims=True)
    i2 = _lane_min(jnp.where(le2 == m2, lane, big))
    e21 = jnp.exp(m2 - m1)
    w1 = p_g / (1.0 + e21)
    w2 = p_g * e21 / (1.0 + e21)
    e1, e2 = i1 - N_EXP_GROUPS, i2 - N_EXP_GROUPS

    onehot = ((lane == e1) | (lane == e2))
    row = lax.broadcasted_iota(I32, (tm, tm), 0)
    colm = lax.broadcasted_iota(I32, (tm, tm), 1)
    lower = jnp.where(row > colm, 1.0, 0.0).astype(BF16)
    before = jnp.dot(lower, jnp.where(onehot, 1.0, 0.0).astype(BF16), preferred_element_type=F32) + base_scr[...]
    r1 = jnp.sum(jnp.where(lane == e1, before, 0.0), -1, keepdims=True)
    r2 = jnp.sum(jnp.where(lane == e2, before, 0.0), -1, keepdims=True)
    base_scr[...] = base_scr[...] + jnp.sum(jnp.where(onehot, 1.0, 0.0), axis=0, keepdims=True)
    cnt_ref[...] = base_scr[...]

    mf_ref[...] = jnp.where(lane == 0.0, w1, jnp.where(lane == 1.0, w2, 0.0))
    mi = jnp.where(lane == 0.0, e1, jnp.where(lane == 1.0, e2, jnp.where(lane == 2.0, r1, jnp.where(lane == 3.0, r2, 0.0))))
    mi_ref[...] = mi.astype(I32)


def _outproj(x, ma, mb, w_bf, g1, norm2_g, sc, sh, wr_hi, wr_lo, br, per_token, tm, seq):
    t = x.shape[0]
    tpb = max(seq // tm, 1)
    row = lambda: pl.BlockSpec((tm, D_MODEL), lambda i: (i, 0))
    full = lambda shape: pl.BlockSpec(shape, lambda i: (0, 0))
    meta = lambda: pl.BlockSpec((tm, LANES), lambda i: (i, 0))
    return pl.pallas_call(
        functools.partial(_outproj_body, tm=tm), grid=(t // tm,),
        in_specs=[row(), row(), row(),
                  pl.BlockSpec((D_MODEL, D_MODEL), lambda i: (0, 0), pipeline_mode=pl.Buffered(1)),
                  _mod_spec(per_token, tm, tpb),
                  full((1, D_MODEL)), _mod_spec(per_token, tm, tpb), _mod_spec(per_token, tm, tpb),
                  full((D_MODEL, LANES)), full((D_MODEL, LANES)), full((1, LANES))],
        out_specs=[row(), row(), meta(), meta(), full((1, LANES))],
        out_shape=[jax.ShapeDtypeStruct((t, D_MODEL), F32), jax.ShapeDtypeStruct((t, D_MODEL), F32),
                   jax.ShapeDtypeStruct((t, LANES), F32), jax.ShapeDtypeStruct((t, LANES), I32),
                   jax.ShapeDtypeStruct((1, LANES), F32)],
        scratch_shapes=[pltpu.VMEM((1, LANES), F32)],
        compiler_params=_params(("arbitrary",)), name="outproj")(
            x, ma, mb, w_bf, g1, norm2_g, sc, sh, wr_hi, wr_lo, br)


def _lane_cumsum(x):
    lane = lax.broadcasted_iota(I32, x.shape, 1)
    for d in (1, 2, 4, 8, 16):
        x = x + jnp.where(lane >= d, pltpu.roll(x, d, 1), 0.0)
    return x


def _plan_body(cnt_ref, mi_ref, dest_ref, bexp_ref, pend_ref, *, tp, blk, nb_pad):
    cnt = jnp.broadcast_to(cnt_ref[...], (SUBLANES, LANES))
    lane8 = lax.broadcasted_iota(I32, (SUBLANES, LANES), 1)
    padded = jnp.where(lane8 < N_EXPERTS, jnp.ceil(cnt / blk) * blk, 0.0)
    pend = _lane_cumsum(padded)
    pstart = (pend - padded)[0:1, :]
    mi = mi_ref[...]
    lane = lax.broadcasted_iota(I32, (tp, LANES), 1)
    e1, e2, r1, r2 = mi[:, 0:1], mi[:, 1:2], mi[:, 2:3], mi[:, 3:4]
    d1 = jnp.sum(jnp.where(lane == e1, pstart, 0.0), -1, keepdims=True).astype(I32) + r1
    d2 = jnp.sum(jnp.where(lane == e2, pstart, 0.0), -1, keepdims=True).astype(I32) + r2
    dest_ref[...] = jnp.where(lane == 0, d1, jnp.where(lane == 1, d2, 0))

    @pl.when(pl.program_id(0) == 0)
    def _():
        jb = lax.broadcasted_iota(I32, (nb_pad, LANES), 0).astype(F32) * blk
        ln = lax.broadcasted_iota(I32, (nb_pad, LANES), 1)
        ends = jnp.where((ln < N_EXPERTS) & (pend[0:1, :] <= jb), 1.0, 0.0)
        be = jnp.minimum(jnp.sum(ends, -1, keepdims=True), N_EXPERTS - 1.0)
        used = pend[0:1, N_EXPERTS - 1:N_EXPERTS] / blk
        bexp_ref[...] = jnp.where(ln == 0, be, jnp.where(ln == 1, used, 0.0)).astype(I32)
        pend_ref[...] = pend.astype(I32)


def _plan(cnt, mi, blk, nb_pad):
    t = mi.shape[0]
    tp = min(t, 1024)
    return pl.pallas_call(
        functools.partial(_plan_body, tp=tp, blk=float(blk), nb_pad=nb_pad), grid=(t // tp,),
        in_specs=[pl.BlockSpec((1, LANES), lambda i: (0, 0)), pl.BlockSpec((tp, LANES), lambda i: (i, 0))],
        out_specs=[pl.BlockSpec((tp, LANES), lambda i: (i, 0)), pl.BlockSpec((nb_pad, LANES), lambda i: (0, 0)),
                   pl.BlockSpec((SUBLANES, LANES), lambda i: (0, 0))],
        out_shape=[jax.ShapeDtypeStruct((t, LANES), I32), jax.ShapeDtypeStruct((nb_pad, LANES), I32),
                   jax.ShapeDtypeStruct((SUBLANES, LANES), I32)],
        compiler_params=_params(("arbitrary",)), name="plan")(cnt, mi)


def _dispatch_body(pend_ref, dest_ref, h2_ref, xs_ref, zbuf, tiles, lsem, rsem, zsem, *, tm, blk, nsteps):
    i = pl.program_id(0)

    def zero_copy(e):
        return pltpu.make_async_copy(zbuf, xs_ref.at[pl.ds(pl.multiple_of(pend_ref[e] - blk, blk), blk)], zsem)

    def has_rows(e):
        return pend_ref[e] > jnp.where(e > 0, pend_ref[jnp.maximum(e - 1, 0)], 0)

    @pl.when(i == 0)
    def _():
        zbuf[...] = jnp.zeros_like(zbuf)

        def zstart(e, c):
            @pl.when(has_rows(e))
            def _():
                zero_copy(e).start()
            return c

        def zwait(e, c):
            @pl.when(has_rows(e))
            def _():
                zero_copy(e).wait()
            return c

        lax.fori_loop(0, N_EXPERTS, zstart, 0)
        lax.fori_loop(0, N_EXPERTS, zwait, 0)

        def tail_copy(j):
            return pltpu.make_async_copy(zbuf, xs_ref.at[pl.ds(pl.multiple_of(j * blk, blk), blk)], zsem)

        def tstart(j, c):
            tail_copy(j).start()
            return c

        def twait(j, c):
            tail_copy(j).wait()
            return c

        used = pend_ref[N_EXPERTS - 1] // blk
        lax.fori_loop(used, xs_ref.shape[0] // blk, tstart, 0)
        lax.fori_loop(used, xs_ref.shape[0] // blk, twait, 0)

    def load(j):
        s = j % DISPATCH_SLOTS
        return pltpu.make_async_copy(h2_ref.at[pl.ds(pl.multiple_of(j * tm, tm), tm)], tiles.at[s], lsem.at[s])

    def wait_rows(j):
        s = j % DISPATCH_SLOTS
        for _ in range(2):
            pltpu.make_async_copy(tiles.at[s], xs_ref.at[pl.ds(0, tm)], rsem.at[s]).wait()

    @pl.when(i == 0)
    def _():
        load(0).start()
        if nsteps > 1:
            load(1).start()

    @pl.when(i >= 2)
    def _():
        wait_rows(i - 2)

    @pl.when(i + 2 < nsteps)
    def _():
        load(i + 2).start()

    load(i).wait()
    slot = i % DISPATCH_SLOTS

    def start(r, c):
        for k in range(2):
            pltpu.make_async_copy(tiles.at[slot, pl.ds(r, 1)], xs_ref.at[pl.ds(dest_ref[k, r], 1)],
                                  rsem.at[slot]).start(priority=k)
        return c

    lax.fori_loop(0, tm, start, 0, unroll=8)

    @pl.when(i == nsteps - 1)
    def _():
        if nsteps > 1:
            wait_rows(i - 1)
        wait_rows(i)


DISPATCH_SLOTS = 4


def _dispatch(pend, dest3, h2, n_rows, tm, blk):
    t = h2.shape[0]
    nsteps = t // tm
    return pl.pallas_call(
        functools.partial(_dispatch_body, tm=tm, blk=blk, nsteps=nsteps), grid=(nsteps,),
        in_specs=[pl.BlockSpec(memory_space=pltpu.SMEM),
                  pl.BlockSpec((None, 2, tm), lambda i: (i, 0, 0), memory_space=pltpu.SMEM),
                  pl.BlockSpec(memory_space=pl.ANY)],
        out_specs=pl.BlockSpec(memory_space=pl.ANY),
        out_shape=jax.ShapeDtypeStruct((n_rows, D_MODEL), F32),
        scratch_shapes=[pltpu.VMEM((blk, D_MODEL), F32), pltpu.VMEM((DISPATCH_SLOTS, tm, D_MODEL), F32),
                        pltpu.SemaphoreType.DMA((DISPATCH_SLOTS,)), pltpu.SemaphoreType.DMA((DISPATCH_SLOTS,)),
                        pltpu.SemaphoreType.DMA(())],
        compiler_params=_params(("arbitrary",)), name="dispatch")(pend, dest3, h2)


def _moe_body(bexp_ref, used_ref, xs_ref, wu_ref, wd_ref, ys_ref, wu_bf, wd_bf):
    j = pl.program_id(0)
    changed = (j == 0) | (bexp_ref[j] != bexp_ref[jnp.maximum(j - 1, 0)])

    @pl.when(changed)
    def _():
        wu_bf[...] = wu_ref[...].astype(BF16)
        wd_bf[...] = wd_ref[...].astype(BF16)

    @pl.when(j < used_ref[0])
    def _():
        up = jnp.dot(xs_ref[...].astype(BF16), wu_bf[...], preferred_element_type=F32)
        act = jax.nn.silu(up[:, :D_EXPERT]) * up[:, D_EXPERT:]
        ys_ref[...] = jnp.dot(act.astype(BF16), wd_bf[...], preferred_element_type=F32)

    @pl.when(j >= used_ref[0])
    def _():
        ys_ref[...] = jnp.zeros_like(ys_ref)


def _moe(bexp, used, xs, w_up, w_down, blk):
    p = xs.shape[0]
    grid_spec = pltpu.PrefetchScalarGridSpec(
        num_scalar_prefetch=2, grid=(p // blk,),
        in_specs=[pl.BlockSpec((blk, D_MODEL), lambda j, be, us: (j, 0)),
                  pl.BlockSpec((None, D_MODEL, 2 * D_EXPERT), lambda j, be, us: (be[j], 0, 0)),
                  pl.BlockSpec((None, D_EXPERT, D_MODEL), lambda j, be, us: (be[j], 0, 0))],
        out_specs=pl.BlockSpec((blk, D_MODEL), lambda j, be, us: (j, 0)),
        scratch_shapes=[pltpu.VMEM((D_MODEL, 2 * D_EXPERT), BF16), pltpu.VMEM((D_EXPERT, D_MODEL), BF16)])
    return pl.pallas_call(
        _moe_body, grid_spec=grid_spec, out_shape=jax.ShapeDtypeStruct((p, D_MODEL), F32),
        compiler_params=_params(("arbitrary",)), name="moe")(bexp, used, xs, w_up, w_down)


def _combine_body(dest_ref, dnext_ref, x1_ref, mf_ref, g2_ref, ys_ref, o_ref, ybuf, sem, *, tm, nsteps):
    i = pl.program_id(0)

    def issue(dref, s):
        def start(r, c):
            for k in range(2):
                pltpu.make_async_copy(ys_ref.at[pl.ds(dref[k, r], 1)], ybuf.at[s, k, pl.ds(r, 1)],
                                      sem.at[s]).start(priority=k)
            return c

        lax.fori_loop(0, tm, start, 0, unroll=8)

    @pl.when(i == 0)
    def _():
        issue(dest_ref, 0)

    for s in range(2):
        @pl.when(i % 2 == s)
        def _(s=s):
            @pl.when(i + 1 < nsteps)
            def _():
                issue(dnext_ref, 1 - s)

            for k in range(2):
                pltpu.make_async_copy(ys_ref.at[pl.ds(0, tm)], ybuf.at[s, k], sem.at[s]).wait()
            mf = mf_ref[...]
            moe = mf[:, 0:1] * ybuf[s, 0] + mf[:, 1:2] * ybuf[s, 1]
            o_ref[...] = x1_ref[...] + g2_ref[...] * moe


def _combine(dest3, x1, mf, g2, ys, per_token, tm, seq):
    t = x1.shape[0]
    tpb = max(seq // tm, 1)
    nsteps = t // tm
    return pl.pallas_call(
        functools.partial(_combine_body, tm=tm, nsteps=nsteps), grid=(nsteps,),
        in_specs=[pl.BlockSpec((None, 2, tm), lambda i: (i, 0, 0), memory_space=pltpu.SMEM),
                  pl.BlockSpec((None, 2, tm), lambda i: (jnp.minimum(i + 1, nsteps - 1), 0, 0),
                               memory_space=pltpu.SMEM),
                  pl.BlockSpec((tm, D_MODEL), lambda i: (i, 0)),
                  pl.BlockSpec((tm, LANES), lambda i: (i, 0)),
                  _mod_spec(per_token, tm, tpb),
                  pl.BlockSpec(memory_space=pl.ANY)],
        out_specs=pl.BlockSpec((tm, D_MODEL), lambda i: (i, 0)),
        out_shape=jax.ShapeDtypeStruct((t, D_MODEL), F32),
        scratch_shapes=[pltpu.VMEM((2, 2, tm, D_MODEL), F32), pltpu.SemaphoreType.DMA((2,))],
        compiler_params=_params(("arbitrary",)), name="combine")(dest3, dest3, x1, mf, g2, ys)


def _group_mod(mod6, per_token, seq):
    if per_token:
        return jnp.repeat(mod6, seq, axis=0)
    return mod6[:, None, :]


def _layer(x, mods, pos, conv_state, h0, k_cache, v_cache, w, seq, tm_in, tm_out, tq, blk, nsub):
    batch = x.shape[0]
    t = batch * seq
    xt = x.reshape(t, D_MODEL)
    per_token = seq < tm_out
    sh1, sc1, g1, sh2, sc2, g2 = [_group_mod(m, per_token, seq) for m in mods]

    proj = _inproj(xt, w["norm1_g"], sc1, sh1, w["w_in"], per_token, tm_in, seq)

    ma, new_h, new_conv = _lru(proj, conv_state, h0, w["conv_w"], w["conv_b"], w["w_ra"], w["b_ra"],
                               w["w_ri"], w["b_ri"], w["lru_lambda"], batch, seq, nsub)

    tabs = _rope_tables(pos)
    keep = min(WINDOW, seq)
    if k_cache is None:
        mb, kn = _attn(w["attn_sink"], proj, tabs, w["q_norm_g"], w["k_norm_g"], None, None, batch, seq, keep)
    else:
        rows = k_cache.shape[1]
        mb, kn = _attn(w["attn_sink"], proj, tabs, w["q_norm_g"], w["k_norm_g"],
                       k_cache.reshape(batch * rows, KV_W), v_cache.reshape(batch * rows, KV_W), batch, seq, keep)
    v3 = proj.reshape(batch, seq, N_IN_COLS)[:, seq - keep:, COL_V:COL_V + KV_W]
    new_k = kn.astype(F32).reshape(batch, keep, N_KV, HEAD_DIM)
    new_v = v3.astype(F32).reshape(batch, keep, N_KV, HEAD_DIM)
    if k_cache is not None:
        new_k = jnp.concatenate([k_cache, new_k], axis=1)[:, -rows:]
        new_v = jnp.concatenate([v_cache, new_v], axis=1)[:, -rows:]

    x1, h2, mf, mi, cnt = _outproj(xt, ma, mb, w["w_out"], g1, w["norm2_g"], sc2, sh2,
                                   w["wr_hi"], w["wr_lo"], w["br"], per_token, tm_out, seq)

    n_blocks = -(-(2 * t) // blk) + N_EXPERTS
    nb_pad = -(-n_blocks // SUBLANES) * SUBLANES
    dest, bmeta, pend = _plan(cnt, mi, blk, nb_pad)
    tmd = min(t, 256)
    dest_d = dest[:, 0:2].reshape(t // tmd, tmd, 2).transpose(0, 2, 1)
    dest_c = dest[:, 0:2].reshape(t // tm_out, tm_out, 2).transpose(0, 2, 1)
    xs = _dispatch(pend[0, :N_EXPERTS], dest_d, h2, n_blocks * blk, tmd, blk)
    ys = _moe(bmeta[:n_blocks, 0], bmeta[0:1, 1], xs, w["w_exp_up"], w["w_exp_down"], blk)
    out = _combine(dest_c, x1, mf, g2, ys, per_token, tm_out, seq)
    return (out.reshape(batch, seq, D_MODEL), new_h.reshape(batch, D_MODEL), new_conv, new_k, new_v)


def kernel(x_prompt, x_sample, state_lru_h, state_lru_conv, cache_swa_k, cache_swa_v, c_prompt, c_sample,
           norm1_g, norm2_g, w_ada, b_ada, w_in, conv_w, conv_b, w_ra, b_ra, w_ri, b_ri, lru_lambda,
           q_norm_g, k_norm_g, attn_sink, w_out, w_router_g, b_router_g, w_router_e, b_router_e,
           w_exp_up, w_exp_down):
    assert norm1_g.shape[0] == 1, "single-layer trunk"
    bp, sp, _ = x_prompt.shape
    bs, ss, _ = x_sample.shape

    mod = _ada(jnp.concatenate([c_prompt, c_sample], axis=0), w_ada[0], b_ada[0])
    mods_p = jnp.split(mod[:bp], 6, axis=-1)
    mods_s = jnp.split(mod[bp:], 6, axis=-1)

    pad = LANES - N_EXP_GROUPS - N_EXPERTS
    wr = jnp.concatenate([w_router_g[0], w_router_e[0], jnp.zeros((D_MODEL, pad), F32)], axis=1)
    wr_hi = wr.astype(BF16)
    wr_lo = (wr - wr_hi.astype(F32)).astype(BF16)
    br = jnp.concatenate([b_router_g[0], b_router_e[0], jnp.zeros((pad,), F32)]).reshape(1, LANES)

    w_in_bf = jnp.concatenate([w_in[0][:, :REF_COL_K], w_in[0][:, REF_COL_GA:], w_in[0][:, REF_COL_K:REF_COL_GA]],
                              axis=1).astype(BF16)
    w = dict(norm1_g=norm1_g[0].reshape(1, D_MODEL), norm2_g=norm2_g[0].reshape(1, D_MODEL),
             w_in=w_in_bf, conv_w=conv_w[0], conv_b=conv_b[0], w_ra=w_ra[0], b_ra=b_ra[0],
             w_ri=w_ri[0], b_ri=b_ri[0], lru_lambda=lru_lambda[0], q_norm_g=q_norm_g[0], k_norm_g=k_norm_g[0],
             attn_sink=attn_sink[0], w_out=w_out[0].astype(BF16), wr_hi=wr_hi, wr_lo=wr_lo, br=br,
             w_exp_up=w_exp_up[0], w_exp_down=w_exp_down[0])

    yp, ph, pc, pk, pv = _layer(x_prompt, mods_p, jnp.arange(sp), jnp.zeros((bp, CONV_W - 1, D_MODEL), F32),
                                jnp.zeros((bp, D_MODEL), F32), None, None, w, sp,
                                tm_in=min(1024, sp), tm_out=min(512, sp), tq=256, blk=256, nsub=2)
    ys, sh, sc, sk, sv = _layer(x_sample, mods_s, PAST_LEN + jnp.arange(ss), state_lru_conv[0], state_lru_h[0],
                                cache_swa_k[0], cache_swa_v[0], w, ss,
                                tm_in=bs * ss, tm_out=bs * ss, tq=bs * ss, blk=128, nsub=LRU_BLOCKS)
    return (yp, ys, ph[None], pc[None], pk[None], pv[None], sh[None], sc[None], sk[None], sv[None])
```

```python
import functools
import math

import jax
import jax.numpy as jnp
from jax import lax
from jax.experimental import pallas as pl
from jax.experimental.pallas import tpu as pltpu

F32 = jnp.float32
BF16 = jnp.bfloat16
I32 = jnp.int32

D_MODEL = 2048
EPS = 1e-6
LANES = 128
SUBLANES = 8
CHUNK = 64
CONV_W = 4
LRU_BLOCKS = 16
LRU_BS = D_MODEL // LRU_BLOCKS
LRU_C = 8.0
HEAD_DIM = 128
N_HEADS = D_MODEL // HEAD_DIM
N_KV = 4
GROUP = N_HEADS // N_KV
WINDOW = 128
ROT_DIMS = HEAD_DIM // 4
ROPE_THETA = 500000.0
NEG = -1e30
N_EXP_GROUPS = 4
EXP_PER_GROUP = 8
N_EXPERTS = N_EXP_GROUPS * EXP_PER_GROUP
D_EXPERT = D_MODEL // 4
Q_W = N_HEADS * HEAD_DIM
KV_W = N_KV * HEAD_DIM
REF_COL_K = 2 * D_MODEL + Q_W
REF_COL_GA = REF_COL_K + 2 * KV_W
COL_XA, COL_YA, COL_Q = 0, D_MODEL, 2 * D_MODEL
COL_GA = COL_Q + Q_W
COL_GB = COL_GA + D_MODEL
COL_K = COL_GB + D_MODEL
COL_V = COL_K + KV_W
N_IN_COLS = COL_V + KV_W
PAST_LEN = 1024
ATT_SCALE = HEAD_DIM ** -0.5
VMEM_LIMIT = 56 * 1024 * 1024


def _params(sem, vmem=VMEM_LIMIT):
    return pltpu.CompilerParams(dimension_semantics=sem, vmem_limit_bytes=vmem)


def _ada_body(c_ref, w_ref, b_ref, o_ref):
    c = c_ref[...]
    s = c * jax.nn.sigmoid(c)
    o_ref[...] = jnp.dot(s.astype(BF16), w_ref[...].astype(BF16), preferred_element_type=F32) + b_ref[...]


def _ada(c, w_ada, b_ada):
    bt, n, tn = c.shape[0], w_ada.shape[1], 1024
    return pl.pallas_call(
        _ada_body, grid=(n // tn,),
        in_specs=[pl.BlockSpec((bt, D_MODEL), lambda j: (0, 0)),
                  pl.BlockSpec((D_MODEL, tn), lambda j: (0, j)),
                  pl.BlockSpec((1, tn), lambda j: (0, j))],
        out_specs=pl.BlockSpec((bt, tn), lambda j: (0, j)),
        out_shape=jax.ShapeDtypeStruct((bt, n), F32),
        compiler_params=_params(("arbitrary",)), name="ada")(c, w_ada, b_ada.reshape(1, n))


def _norm_mod(x, g, sc, sh):
    ms = jnp.mean(x * x, axis=-1, keepdims=True)
    return (x * lax.rsqrt(ms + EPS) * g) * (1.0 + sc) + sh


def _inproj_body(x_ref, g_ref, sc_ref, sh_ref, w_ref, o_ref, h_scr):
    @pl.when(pl.program_id(1) == 0)
    def _():
        h_scr[...] = _norm_mod(x_ref[...], g_ref[...], sc_ref[...], sh_ref[...]).astype(BF16)

    o_ref[...] = jnp.dot(h_scr[...], w_ref[...], preferred_element_type=F32).astype(o_ref.dtype)


def _mod_spec(per_token, tm, tiles_per_batch):
    if per_token:
        return pl.BlockSpec((tm, D_MODEL), lambda i, *_: (i, 0))
    return pl.BlockSpec((None, 1, D_MODEL), lambda i, *_: (i // tiles_per_batch, 0, 0))


def _ref_col_tile(j, tn):
    n_head, n_gate, n_kv = REF_COL_K // tn, (COL_K - COL_GA) // tn, (2 * KV_W) // tn
    assert n_head * tn == REF_COL_K and n_gate * tn == COL_K - COL_GA and n_kv * tn == 2 * KV_W
    return jnp.where(j < n_head, j, jnp.where(j < n_head + n_gate, j + n_kv, j - n_gate))


def _inproj(x, norm_g, sc, sh, w_bf, per_token, tm, seq):
    t, tn = x.shape[0], 1024
    tpb = max(seq // tm, 1)
    return pl.pallas_call(
        _inproj_body, grid=(t // tm, N_IN_COLS // tn),
        in_specs=[pl.BlockSpec((tm, D_MODEL), lambda i, j: (i, 0)),
                  pl.BlockSpec((1, D_MODEL), lambda i, j: (0, 0)),
                  _mod_spec(per_token, tm, tpb), _mod_spec(per_token, tm, tpb),
                  pl.BlockSpec((D_MODEL, tn), lambda i, j: (0, _ref_col_tile(j, tn)))],
        out_specs=pl.BlockSpec((tm, tn), lambda i, j: (i, j)),
        out_shape=jax.ShapeDtypeStruct((t, N_IN_COLS), BF16),
        scratch_shapes=[pltpu.VMEM((tm, D_MODEL), BF16)],
        compiler_params=_params(("arbitrary", "arbitrary")), name="inproj")(x, norm_g, sc, sh, w_bf)


def _norm_rope_all(xs, gains, c, s1, s2):
    ms = [jnp.mean(x * x, axis=-1, keepdims=True) for x in xs]
    ys = [x * lax.rsqrt(m + EPS) * g for x, m, g in zip(xs, ms, gains)]
    up = [pltpu.roll(y, ROT_DIMS // 2, 1) for y in ys]
    dn = [pltpu.roll(y, HEAD_DIM - ROT_DIMS // 2, 1) for y in ys]
    return [(y * c + u * s1 + d * s2).astype(BF16) for y, u, d in zip(ys, up, dn)]


def _rope_tables(pos):
    half = ROT_DIMS // 2
    inv = ROPE_THETA ** (-2.0 * jnp.arange(half, dtype=F32) / ROT_DIMS)
    ang = pos.astype(F32)[:, None] * inv[None, :]
    cos, sin = jnp.cos(ang), jnp.sin(ang)
    n = pos.shape[0]
    rest = HEAD_DIM - ROT_DIMS
    c = jnp.concatenate([cos, cos, jnp.ones((n, rest), F32)], axis=1)
    s1 = jnp.concatenate([jnp.zeros((n, half), F32), sin, jnp.zeros((n, rest), F32)], axis=1)
    s2 = jnp.concatenate([-sin, jnp.zeros((n, half + rest), F32)], axis=1)
    return c, s1, s2


def _lru_body(xa_ref, ya_ref, ga_ref, cs_ref, h0_ref, cw_ref, cb_ref, wra_ref, bra_ref, wri_ref, bri_ref,
              lam_ref, o_ref, nh_ref, nc_ref, xpad, a_scr, b_scr, g_scr, *, seq, rc, nsub):
    pad = SUBLANES
    xpad[0:pad, :] = jnp.zeros((pad, nsub * LRU_BS), F32)
    xpad[pad - (CONV_W - 1):pad, :] = cs_ref[...]
    xpad[pad:pad + seq, :] = xa_ref[...].astype(F32)
    nc_ref[...] = xpad[seq + pad - (CONV_W - 1):seq + pad, :]

    z = -lam_ref[...]
    softplus = jnp.maximum(z, 0.0) + jnp.log1p(jnp.exp(-jnp.abs(z)))
    clam_all = -LRU_C * softplus
    for sb in range(nsub):
        ls = slice(sb * LRU_BS, (sb + 1) * LRU_BS)
        clam = clam_all[:, ls]
        wra = wra_ref[sb].astype(BF16)
        wri = wri_ref[sb].astype(BF16)
        for c0 in range(0, seq, rc):
            xc = cb_ref[:, ls]
            for tap in range(CONV_W):
                r0 = c0 + pad - (CONV_W - 1) + tap
                xc = xc + xpad[r0:r0 + rc, ls] * cw_ref[tap:tap + 1, ls]
            xcb = xc.astype(BF16)
            r = jax.nn.sigmoid(jnp.dot(xcb, wra, preferred_element_type=F32) + bra_ref[sb])
            ig = jax.nn.sigmoid(jnp.dot(xcb, wri, preferred_element_type=F32) + bri_ref[sb])
            log_a = r * clam
            a_scr[c0:c0 + rc, ls] = jnp.exp(log_a)
            th = jnp.tanh(log_a)
            mult = jnp.sqrt(jnp.maximum(-2.0 * th / (1.0 - th), 0.0))
            b_scr[c0:c0 + rc, ls] = mult * (ig * xc)
            ya = ya_ref[c0:c0 + rc, ls].astype(F32)
            ga = ga_ref[c0:c0 + rc, ls].astype(F32)
            g_scr[c0:c0 + rc, ls] = jax.nn.sigmoid(ga) * jax.nn.gelu(ya)

    rows = 2 * SUBLANES
    rid = lax.broadcasted_iota(I32, (SUBLANES, LRU_BS), 0)

    def step(it, carry):
        r0 = pl.multiple_of(it * rows, rows)
        out = []
        for sb in range(nsub):
            ls = slice(sb * LRU_BS, (sb + 1) * LRU_BS)
            c = carry[:, ls]
            hs = []
            for half in range(rows // SUBLANES):
                rh = pl.multiple_of(r0 + half * SUBLANES, SUBLANES)
                a = a_scr[pl.ds(rh, SUBLANES), ls]
                b = b_scr[pl.ds(rh, SUBLANES), ls]
                for d in (1, 2, 4):
                    keep = rid >= d
                    a_s = jnp.where(keep, pltpu.roll(a, d, 0), 1.0)
                    b_s = jnp.where(keep, pltpu.roll(b, d, 0), 0.0)
                    b = a * b_s + b
                    a = a * a_s
                hs.append(b + a * c)
                c = a[SUBLANES - 1:SUBLANES, :] * c + b[SUBLANES - 1:SUBLANES, :]
            out.append(c)
            h = jnp.concatenate(hs, axis=0)
            o_ref[pl.ds(r0, rows), ls] = (g_scr[pl.ds(r0, rows), ls] * h).astype(o_ref.dtype)
        return jnp.concatenate(out, axis=1) if nsub > 1 else out[0]

    nh_ref[...] = lax.fori_loop(0, seq // rows, step, h0_ref[...], unroll=min(2, seq // rows))


def _lru(proj, conv_state, h0, conv_w, conv_b, w_ra, b_ra, w_ri, b_ri, lam, batch, seq, nsub):
    t = proj.shape[0]
    rc = min(seq, 256)
    cw = nsub * LRU_BS
    col = lambda off: (lambda b, c: (b, off // cw + c))
    vec = lambda: pl.BlockSpec((1, cw), lambda b, c: (0, c))
    blk3 = lambda r: pl.BlockSpec((None, r, cw), lambda b, c: (b, 0, c))
    wblk = lambda: pl.BlockSpec((nsub, LRU_BS, LRU_BS), lambda b, c: (c, 0, 0))
    bblk = lambda: pl.BlockSpec((nsub, 1, LRU_BS), lambda b, c: (c, 0, 0))
    return pl.pallas_call(
        functools.partial(_lru_body, seq=seq, rc=rc, nsub=nsub), grid=(batch, LRU_BLOCKS // nsub),
        in_specs=[pl.BlockSpec((seq, cw), col(COL_XA)), pl.BlockSpec((seq, cw), col(COL_YA)),
                  pl.BlockSpec((seq, cw), col(COL_GA)), blk3(CONV_W - 1), blk3(1),
                  pl.BlockSpec((CONV_W, cw), lambda b, c: (0, c)), vec(), wblk(), bblk(), wblk(), bblk(), vec()],
        out_specs=[pl.BlockSpec((seq, cw), lambda b, c: (b, c)), blk3(1), blk3(CONV_W - 1)],
        out_shape=[jax.ShapeDtypeStruct((t, D_MODEL), BF16), jax.ShapeDtypeStruct((batch, 1, D_MODEL), F32),
                   jax.ShapeDtypeStruct((batch, CONV_W - 1, D_MODEL), F32)],
        scratch_shapes=[pltpu.VMEM((seq + SUBLANES, cw), F32), pltpu.VMEM((seq, cw), F32),
                        pltpu.VMEM((seq, cw), F32), pltpu.VMEM((seq, cw), F32)],
        compiler_params=_params(("arbitrary", "arbitrary")), name="lru")(
            proj, proj, proj, conv_state, h0.reshape(batch, 1, D_MODEL), conv_w, conv_b.reshape(1, D_MODEL),
            w_ra, b_ra.reshape(LRU_BLOCKS, 1, LRU_BS), w_ri, b_ri.reshape(LRU_BLOCKS, 1, LRU_BS),
            lam.reshape(1, D_MODEL))


def _attn_body(*refs, banded, qt, npv):
    if banded:
        (sink_ref, q_ref, kc_ref, vp_ref, vc_ref, gb_ref, qg_ref, kg_ref, c_ref, s1_ref, s2_ref,
         o_ref, kn_ref, kprev) = refs
    else:
        (sink_ref, q_ref, kp_ref, kc_ref, vp_ref, vc_ref, gb_ref, qg_ref, kg_ref, c_ref, s1_ref, s2_ref,
         o_ref, kn_ref) = refs
    nt = (((1,), (1,)), ((), ()))
    rows = GROUP * qt
    merged_keys = banded
    i = pl.program_id(1)
    if banded:
        nk = npv + qt
        qc = (lax.broadcasted_iota(I32, (rows, nk), 0) % qt) // CHUNK
        col = lax.broadcasted_iota(I32, (rows, nk), 1)
        valid = ((col < npv) & (col // CHUNK >= qc) & (i > 0)) | ((col >= npv) & ((col - npv) // CHUNK <= qc))

        @pl.when(i == 0)
        def _():
            kprev[...] = jnp.zeros_like(kprev)

    tab = (c_ref[...], s1_ref[...], s2_ref[...])
    log2e = math.log2(math.e)
    slabs = ([q_ref[:, h * HEAD_DIM:(h + 1) * HEAD_DIM].astype(F32) for h in range(N_HEADS)]
             + [kc_ref[:, g * HEAD_DIM:(g + 1) * HEAD_DIM].astype(F32) for g in range(N_KV)])
    normed = _norm_rope_all(slabs, [qg_ref[...]] * N_HEADS + [kg_ref[...]] * N_KV, *tab)
    groups = []
    for g in range(N_KV):
        gs = slice(g * HEAD_DIM, (g + 1) * HEAD_DIM)
        heads = [g * GROUP + hh for hh in range(GROUP)]
        q = jnp.concatenate([normed[h] for h in heads], axis=0)
        sk = jnp.concatenate([jnp.full((qt, 1), sink_ref[h] * log2e, F32) for h in heads], axis=0)
        kc = normed[N_HEADS + g]
        kn_ref[:, gs] = kc
        if banded:
            kp = kprev[:, gs]
            kprev[:, gs] = kc
        else:
            kp = kp_ref[:, gs].astype(BF16)
        vp, vc = vp_ref[:, gs].astype(BF16), vc_ref[:, gs].astype(BF16)
        if merged_keys:
            parts = [(jnp.concatenate([kp, kc], axis=0), jnp.concatenate([vp, vc], axis=0), valid)]
        else:
            parts = [(kp, vp, None), (kc, vc, None)]
        parts = [(k, jnp.concatenate([v, jnp.ones(v.shape, BF16)], axis=1), ok) for k, v, ok in parts]
        groups.append(dict(heads=heads, q=q, sk=sk, parts=parts))

    def logits2(q, k, ok):
        s = lax.dot_general(q, k, nt, preferred_element_type=F32) * (ATT_SCALE * log2e)
        return s if ok is None else jnp.where(ok, s, NEG)

    for gr in groups:
        gr["s"] = [logits2(gr["q"], k, ok) for k, _, ok in gr["parts"]]
    for gr in groups:
        m = gr["sk"]
        for s in gr["s"]:
            m = jnp.maximum(m, jnp.max(s, -1, keepdims=True))
        gr["m"] = m
    for gr in groups:
        gr["p"] = [jnp.exp2(s - gr["m"]).astype(BF16) for s in gr["s"]]
    for gr in groups:
        acc = jnp.dot(gr["p"][0], gr["parts"][0][1], preferred_element_type=F32)
        for p, (_, v, _) in zip(gr["p"][1:], gr["parts"][1:]):
            acc = acc + jnp.dot(p, v, preferred_element_type=F32)
        den = acc[:, HEAD_DIM:] + jnp.exp2(gr["sk"] - gr["m"])
        gr["o"] = acc[:, :HEAD_DIM] / den
    for gr in groups:
        for hh, h in enumerate(gr["heads"]):
            hs = slice(h * HEAD_DIM, (h + 1) * HEAD_DIM)
            gate = jax.nn.sigmoid(gb_ref[:, hs].astype(F32))
            o_ref[:, hs] = (gate * gr["o"][hh * qt:(hh + 1) * qt]).astype(o_ref.dtype)


def _attn(sink, proj, tabs, q_norm_g, k_norm_g, k_cache2, v_cache2, batch, seq, qt):
    t = proj.shape[0]
    nq = seq // qt
    banded = k_cache2 is None
    row = lambda width, col: pl.BlockSpec((qt, width), lambda b, i: (b * nq + i, col))
    vec = lambda: pl.BlockSpec((1, HEAD_DIM), lambda b, i: (0, 0))
    tab = lambda: pl.BlockSpec((qt, HEAD_DIM), lambda b, i: (i, 0))
    if banded:
        npv = qt
        prev_specs = [pl.BlockSpec((npv, KV_W), lambda b, i: (b * nq + jnp.maximum(i - 1, 0), COL_V // KV_W))]
        prev_args = [proj]
        scratch = [pltpu.VMEM((qt, KV_W), BF16)]
    else:
        npv = k_cache2.shape[0] // batch
        prev_specs = [pl.BlockSpec((npv, KV_W), lambda b, i: (b, 0))] * 2
        prev_args = [k_cache2, v_cache2]
        scratch = []
    in_specs = ([pl.BlockSpec(memory_space=pltpu.SMEM), row(Q_W, COL_Q // Q_W)]
                + prev_specs[:-1] + [row(KV_W, COL_K // KV_W), prev_specs[-1], row(KV_W, COL_V // KV_W),
                                     row(Q_W, COL_GB // Q_W), vec(), vec(), tab(), tab(), tab()])
    args = ([sink, proj] + prev_args[:-1] + [proj, prev_args[-1], proj, proj,
                                             q_norm_g.reshape(1, HEAD_DIM), k_norm_g.reshape(1, HEAD_DIM), *tabs])
    return pl.pallas_call(
        functools.partial(_attn_body, banded=banded, qt=qt, npv=npv), grid=(batch, nq),
        in_specs=in_specs,
        out_specs=[pl.BlockSpec((qt, Q_W), lambda b, i: (b * nq + i, 0)),
                   pl.BlockSpec((None, qt, KV_W), lambda b, i: (b, 0, 0))],
        out_shape=[jax.ShapeDtypeStruct((t, Q_W), BF16), jax.ShapeDtypeStruct((batch, qt, KV_W), BF16)],
        scratch_shapes=scratch,
        compiler_params=_params(("arbitrary", "arbitrary")), name="attn")(*args)


def _lane_min(x):
    return jnp.min(x, axis=-1, keepdims=True)


def _outproj_body(x_ref, ma_ref, mb_ref, w_ref, g1_ref, n2_ref, sc_ref, sh_ref, wrh_ref, wrl_ref, br_ref, cin_ref,
                  x1_ref, h2_ref, mf_ref, mi_ref, cnt_ref, base_scr, *, tm):
    @pl.when(pl.program_id(0) == 0)
    def _():
        base_scr[...] = cin_ref[...]

    merged = (ma_ref[...].astype(F32) + mb_ref[...].astype(F32)).astype(BF16)
    x1 = x_ref[...] + g1_ref[...] * jnp.dot(merged, w_ref[...], preferred_element_type=F32)
    x1_ref[...] = x1
    h2 = _norm_mod(x1, n2_ref[...], sc_ref[...], sh_ref[...])
    h2_ref[...] = h2

    hi = h2.astype(BF16)
    lo = (h2 - hi.astype(F32)).astype(BF16)
    logits = (jnp.dot(hi, wrh_ref[...], preferred_element_type=F32)
              + jnp.dot(lo, wrh_ref[...], preferred_element_type=F32)
              + jnp.dot(hi, wrl_ref[...], preferred_element_type=F32)) + br_ref[...]
    lane = lax.broadcasted_iota(I32, (tm, LANES), 1).astype(F32)
    ninf = -jnp.inf
    big = float(LANES)
    is_g = lane < N_EXP_GROUPS
    lg = jnp.where(is_g, logits, ninf)
    mg = jnp.max(lg, -1, keepdims=True)
    gsel = _lane_min(jnp.where(lg == mg, lane, big))
    p_g = 1.0 / jnp.sum(jnp.where(is_g, jnp.exp(lg - mg), 0.0), -1, keepdims=True)
    lo_lane = N_EXP_GROUPS + EXP_PER_GROUP * gsel
    le = jnp.where((lane >= lo_lane) & (lane < lo_lane + EXP_PER_GROUP), logits, ninf)
    m1 = jnp.max(le, -1, keepdims=True)
    i1 = _lane_min(jnp.where(le == m1, lane, big))
    le2 = jnp.where(lane == i1, ninf, le)
    m2 = jnp.max(le2, -1, keepdims=True)
    i2 = _lane_min(jnp.where(le2 == m2, lane, big))
    e21 = jnp.exp(m2 - m1)
    w1 = p_g / (1.0 + e21)
    w2 = p_g * e21 / (1.0 + e21)
    e1, e2 = i1 - N_EXP_GROUPS, i2 - N_EXP_GROUPS

    onehot = ((lane == e1) | (lane == e2))
    row = lax.broadcasted_iota(I32, (tm, tm), 0)
    colm = lax.broadcasted_iota(I32, (tm, tm), 1)
    lower = jnp.where(row > colm, 1.0, 0.0).astype(BF16)
    before = jnp.dot(lower, jnp.where(onehot, 1.0, 0.0).astype(BF16), preferred_element_type=F32) + base_scr[...]
    r1 = jnp.sum(jnp.where(lane == e1, before, 0.0), -1, keepdims=True)
    r2 = jnp.sum(jnp.where(lane == e2, before, 0.0), -1, keepdims=True)
    base_scr[...] = base_scr[...] + jnp.sum(jnp.where(onehot, 1.0, 0.0), axis=0, keepdims=True)
    cnt_ref[...] = base_scr[...]

    mf_ref[...] = jnp.where(lane == 0.0, w1, jnp.where(lane == 1.0, w2, 0.0))
    mi = jnp.where(lane == 0.0, e1, jnp.where(lane == 1.0, e2, jnp.where(lane == 2.0, r1, jnp.where(lane == 3.0, r2, 0.0))))
    mi_ref[...] = mi.astype(I32)


def _outproj(x, ma, mb, w_bf, g1, norm2_g, sc, sh, wr_hi, wr_lo, br, cnt_in, per_token, tm, seq):
    t = x.shape[0]
    tpb = max(seq // tm, 1)
    row = lambda: pl.BlockSpec((tm, D_MODEL), lambda i: (i, 0))
    full = lambda shape: pl.BlockSpec(shape, lambda i: (0, 0))
    meta = lambda: pl.BlockSpec((tm, LANES), lambda i: (i, 0))
    return pl.pallas_call(
        functools.partial(_outproj_body, tm=tm), grid=(t // tm,),
        in_specs=[row(), row(), row(),
                  pl.BlockSpec((D_MODEL, D_MODEL), lambda i: (0, 0), pipeline_mode=pl.Buffered(1)),
                  _mod_spec(per_token, tm, tpb),
                  full((1, D_MODEL)), _mod_spec(per_token, tm, tpb), _mod_spec(per_token, tm, tpb),
                  full((D_MODEL, LANES)), full((D_MODEL, LANES)), full((1, LANES)), full((1, LANES))],
        out_specs=[row(), row(), meta(), meta(), full((1, LANES))],
        out_shape=[jax.ShapeDtypeStruct((t, D_MODEL), F32), jax.ShapeDtypeStruct((t, D_MODEL), F32),
                   jax.ShapeDtypeStruct((t, LANES), F32), jax.ShapeDtypeStruct((t, LANES), I32),
                   jax.ShapeDtypeStruct((1, LANES), F32)],
        scratch_shapes=[pltpu.VMEM((1, LANES), F32)],
        compiler_params=_params(("arbitrary",)), name="outproj")(
            x, ma, mb, w_bf, g1, norm2_g, sc, sh, wr_hi, wr_lo, br, cnt_in)


def _lane_cumsum(x):
    lane = lax.broadcasted_iota(I32, x.shape, 1)
    for d in (1, 2, 4, 8, 16):
        x = x + jnp.where(lane >= d, pltpu.roll(x, d, 1), 0.0)
    return x


def _plan_body(cnt_ref, mi_ref, dest_ref, bexp_ref, pend_ref, *, tp, blk, nb_pad):
    cnt = jnp.broadcast_to(cnt_ref[...], (SUBLANES, LANES))
    lane8 = lax.broadcasted_iota(I32, (SUBLANES, LANES), 1)
    padded = jnp.where(lane8 < N_EXPERTS, jnp.ceil(cnt / blk) * blk, 0.0)
    pend = _lane_cumsum(padded)
    pstart = (pend - padded)[0:1, :]
    mi = mi_ref[...]
    lane = lax.broadcasted_iota(I32, (tp, LANES), 1)
    e1, e2, r1, r2 = mi[:, 0:1], mi[:, 1:2], mi[:, 2:3], mi[:, 3:4]
    d1 = jnp.sum(jnp.where(lane == e1, pstart, 0.0), -1, keepdims=True).astype(I32) + r1
    d2 = jnp.sum(jnp.where(lane == e2, pstart, 0.0), -1, keepdims=True).astype(I32) + r2
    dest_ref[...] = jnp.where(lane == 0, d1, jnp.where(lane == 1, d2, 0))

    @pl.when(pl.program_id(0) == 0)
    def _():
        jb = lax.broadcasted_iota(I32, (nb_pad, LANES), 0).astype(F32) * blk
        ln = lax.broadcasted_iota(I32, (nb_pad, LANES), 1)
        ends = jnp.where((ln < N_EXPERTS) & (pend[0:1, :] <= jb), 1.0, 0.0)
        be = jnp.minimum(jnp.sum(ends, -1, keepdims=True), N_EXPERTS - 1.0)
        used = pend[0:1, N_EXPERTS - 1:N_EXPERTS] / blk
        bexp_ref[...] = jnp.where(ln == 0, be, jnp.where(ln == 1, used, 0.0)).astype(I32)
        pend_ref[...] = pend.astype(I32)


def _plan(cnt, mi, blk, nb_pad):
    t = mi.shape[0]
    tp = min(t, 1024)
    return pl.pallas_call(
        functools.partial(_plan_body, tp=tp, blk=float(blk), nb_pad=nb_pad), grid=(t // tp,),
        in_specs=[pl.BlockSpec((1, LANES), lambda i: (0, 0)), pl.BlockSpec((tp, LANES), lambda i: (i, 0))],
        out_specs=[pl.BlockSpec((tp, LANES), lambda i: (i, 0)), pl.BlockSpec((nb_pad, LANES), lambda i: (0, 0)),
                   pl.BlockSpec((SUBLANES, LANES), lambda i: (0, 0))],
        out_shape=[jax.ShapeDtypeStruct((t, LANES), I32), jax.ShapeDtypeStruct((nb_pad, LANES), I32),
                   jax.ShapeDtypeStruct((SUBLANES, LANES), I32)],
        compiler_params=_params(("arbitrary",)), name="plan")(cnt, mi)


DISPATCH_SLOTS = 4


def _dispatch_body(*refs, tm, blk, bounds):
    pend_ref, dest_ref = refs[:2]
    h2_refs = refs[2:2 + len(bounds)]
    xs_ref, zbuf, tiles, lsem, rsem, zsem = refs[2 + len(bounds):]
    nsteps = bounds[-1][1]
    i = pl.program_id(0)

    def zero_copy(e):
        return pltpu.make_async_copy(zbuf, xs_ref.at[pl.ds(pl.multiple_of(pend_ref[e] - blk, blk), blk)], zsem)

    def has_rows(e):
        return pend_ref[e] > jnp.where(e > 0, pend_ref[jnp.maximum(e - 1, 0)], 0)

    @pl.when(i == 0)
    def _():
        zbuf[...] = jnp.zeros_like(zbuf)

        def zstart(e, c):
            @pl.when(has_rows(e))
            def _():
                zero_copy(e).start()
            return c

        def zwait(e, c):
            @pl.when(has_rows(e))
            def _():
                zero_copy(e).wait()
            return c

        lax.fori_loop(0, N_EXPERTS, zstart, 0)
        lax.fori_loop(0, N_EXPERTS, zwait, 0)

        def tail_copy(j):
            return pltpu.make_async_copy(zbuf, xs_ref.at[pl.ds(pl.multiple_of(j * blk, blk), blk)], zsem)

        def tstart(j, c):
            tail_copy(j).start()
            return c

        def twait(j, c):
            tail_copy(j).wait()
            return c

        used = pend_ref[N_EXPERTS - 1] // blk
        lax.fori_loop(used, xs_ref.shape[0] // blk, tstart, 0)
        lax.fori_loop(used, xs_ref.shape[0] // blk, twait, 0)

    def load_start(j):
        s = j % DISPATCH_SLOTS
        for (lo, hi), h2_ref in zip(bounds, h2_refs):
            @pl.when((j >= lo) & (j < hi))
            def _(lo=lo, h2_ref=h2_ref):
                pltpu.make_async_copy(h2_ref.at[pl.ds(pl.multiple_of((j - lo) * tm, tm), tm)], tiles.at[s],
                                      lsem.at[s]).start()

    def load_wait(j):
        s = j % DISPATCH_SLOTS
        pltpu.make_async_copy(h2_refs[0].at[pl.ds(0, tm)], tiles.at[s], lsem.at[s]).wait()

    def wait_rows(j):
        s = j % DISPATCH_SLOTS
        for _ in range(2):
            pltpu.make_async_copy(tiles.at[s], xs_ref.at[pl.ds(0, tm)], rsem.at[s]).wait()

    @pl.when(i == 0)
    def _():
        load_start(0)
        if nsteps > 1:
            load_start(1)

    @pl.when(i >= 2)
    def _():
        wait_rows(i - 2)

    @pl.when(i + 2 < nsteps)
    def _():
        load_start(i + 2)

    load_wait(i)
    slot = i % DISPATCH_SLOTS

    def start(r, c):
        for k in range(2):
            pltpu.make_async_copy(tiles.at[slot, pl.ds(r, 1)], xs_ref.at[pl.ds(dest_ref[0, k * tm + r], 1)],
                                  rsem.at[slot]).start(priority=k)
        return c

    lax.fori_loop(0, tm, start, 0, unroll=8)

    @pl.when(i == nsteps - 1)
    def _():
        if nsteps > 1:
            wait_rows(i - 1)
        wait_rows(i)


def _dispatch(pend, dest3, h2_list, n_rows, tm, blk):
    bounds, lo = [], 0
    for h2 in h2_list:
        bounds.append((lo, lo + h2.shape[0] // tm))
        lo = bounds[-1][1]
    return pl.pallas_call(
        functools.partial(_dispatch_body, tm=tm, blk=blk, bounds=tuple(bounds)), grid=(lo,),
        in_specs=[pl.BlockSpec(memory_space=pltpu.SMEM),
                  pl.BlockSpec((None, 1, 2 * tm), lambda i: (i, 0, 0), memory_space=pltpu.SMEM)]
                 + [pl.BlockSpec(memory_space=pl.ANY)] * len(h2_list),
        out_specs=pl.BlockSpec(memory_space=pl.ANY),
        out_shape=jax.ShapeDtypeStruct((n_rows, D_MODEL), F32),
        scratch_shapes=[pltpu.VMEM((blk, D_MODEL), F32), pltpu.VMEM((DISPATCH_SLOTS, tm, D_MODEL), F32),
                        pltpu.SemaphoreType.DMA((DISPATCH_SLOTS,)), pltpu.SemaphoreType.DMA((DISPATCH_SLOTS,)),
                        pltpu.SemaphoreType.DMA(())],
        compiler_params=_params(("arbitrary",)), name="dispatch")(pend, dest3, *h2_list)


def _moe_body(bexp_ref, used_ref, xs_ref, wu_ref, wd_ref, ys_ref, wu_bf, wd_bf):
    j = pl.program_id(0)
    changed = (j == 0) | (bexp_ref[j] != bexp_ref[jnp.maximum(j - 1, 0)])

    @pl.when(changed)
    def _():
        wu_bf[...] = wu_ref[...].astype(BF16)
        wd_bf[...] = wd_ref[...].astype(BF16)

    @pl.when(j < used_ref[0])
    def _():
        up = jnp.dot(xs_ref[...].astype(BF16), wu_bf[...], preferred_element_type=F32)
        act = jax.nn.silu(up[:, :D_EXPERT]) * up[:, D_EXPERT:]
        ys_ref[...] = jnp.dot(act.astype(BF16), wd_bf[...], preferred_element_type=F32)

    @pl.when(j >= used_ref[0])
    def _():
        ys_ref[...] = jnp.zeros_like(ys_ref)


def _moe(bexp, used, xs, w_up, w_down, blk):
    p = xs.shape[0]
    grid_spec = pltpu.PrefetchScalarGridSpec(
        num_scalar_prefetch=2, grid=(p // blk,),
        in_specs=[pl.BlockSpec((blk, D_MODEL), lambda j, be, us: (j, 0)),
                  pl.BlockSpec((None, D_MODEL, 2 * D_EXPERT), lambda j, be, us: (be[j], 0, 0)),
                  pl.BlockSpec((None, D_EXPERT, D_MODEL), lambda j, be, us: (be[j], 0, 0))],
        out_specs=pl.BlockSpec((blk, D_MODEL), lambda j, be, us: (j, 0)),
        scratch_shapes=[pltpu.VMEM((D_MODEL, 2 * D_EXPERT), BF16), pltpu.VMEM((D_EXPERT, D_MODEL), BF16)])
    return pl.pallas_call(
        _moe_body, grid_spec=grid_spec, out_shape=jax.ShapeDtypeStruct((p, D_MODEL), F32),
        compiler_params=_params(("arbitrary",)), name="moe")(bexp, used, xs, w_up, w_down)


def _combine_body(dest_ref, dnext_ref, x1_ref, mf_ref, g2_ref, ys_ref, o_ref, ybuf, sem, *, tm, nsteps):
    i = pl.program_id(0)

    def issue(dref, s):
        def start(r, c):
            for k in range(2):
                pltpu.make_async_copy(ys_ref.at[pl.ds(dref[0, k * tm + r], 1)], ybuf.at[s, k, pl.ds(r, 1)],
                                      sem.at[s]).start(priority=k)
            return c

        lax.fori_loop(0, tm, start, 0, unroll=8)

    @pl.when(i == 0)
    def _():
        issue(dest_ref, 0)

    for s in range(2):
        @pl.when(i % 2 == s)
        def _(s=s):
            @pl.when(i + 1 < nsteps)
            def _():
                issue(dnext_ref, 1 - s)

            for k in range(2):
                pltpu.make_async_copy(ys_ref.at[pl.ds(0, tm)], ybuf.at[s, k], sem.at[s]).wait()
            mf = mf_ref[...]
            moe = mf[:, 0:1] * ybuf[s, 0] + mf[:, 1:2] * ybuf[s, 1]
            o_ref[...] = x1_ref[...] + g2_ref[...] * moe


def _combine(dest3, x1, mf, g2, ys, per_token, tm, seq):
    t = x1.shape[0]
    tpb = max(seq // tm, 1)
    nsteps = t // tm
    return pl.pallas_call(
        functools.partial(_combine_body, tm=tm, nsteps=nsteps), grid=(nsteps,),
        in_specs=[pl.BlockSpec((None, 1, 2 * tm), lambda i: (i, 0, 0), memory_space=pltpu.SMEM),
                  pl.BlockSpec((None, 1, 2 * tm), lambda i: (jnp.minimum(i + 1, nsteps - 1), 0, 0),
                               memory_space=pltpu.SMEM),
                  pl.BlockSpec((tm, D_MODEL), lambda i: (i, 0)),
                  pl.BlockSpec((tm, LANES), lambda i: (i, 0)),
                  _mod_spec(per_token, tm, tpb),
                  pl.BlockSpec(memory_space=pl.ANY)],
        out_specs=pl.BlockSpec((tm, D_MODEL), lambda i: (i, 0)),
        out_shape=jax.ShapeDtypeStruct((t, D_MODEL), F32),
        scratch_shapes=[pltpu.VMEM((2, 2, tm, D_MODEL), F32), pltpu.SemaphoreType.DMA((2,))],
        compiler_params=_params(("arbitrary",)), name="combine")(dest3, dest3, x1, mf, g2, ys)


def _group_mod(mod6, per_token, seq):
    if per_token:
        return jnp.repeat(mod6, seq, axis=0)
    return mod6[:, None, :]


def _front(x, mods, pos, conv_state, h0, k_cache, v_cache, w, cnt_in, seq, tm_in, tm_out, nsub):
    batch = x.shape[0]
    t = batch * seq
    xt = x.reshape(t, D_MODEL)
    per_token = seq < tm_out
    sh1, sc1, g1, sh2, sc2, g2 = [_group_mod(m, per_token, seq) for m in mods]

    proj = _inproj(xt, w["norm1_g"], sc1, sh1, w["w_in"], per_token, tm_in, seq)

    ma, new_h, new_conv = _lru(proj, conv_state, h0, w["conv_w"], w["conv_b"], w["w_ra"], w["b_ra"],
                               w["w_ri"], w["b_ri"], w["lru_lambda"], batch, seq, nsub)

    tabs = _rope_tables(pos)
    keep = min(WINDOW, seq)
    if k_cache is None:
        mb, kn = _attn(w["attn_sink"], proj, tabs, w["q_norm_g"], w["k_norm_g"], None, None, batch, seq, keep)
    else:
        rows = k_cache.shape[1]
        mb, kn = _attn(w["attn_sink"], proj, tabs, w["q_norm_g"], w["k_norm_g"],
                       k_cache.reshape(batch * rows, KV_W), v_cache.reshape(batch * rows, KV_W), batch, seq, keep)
    v3 = proj.reshape(batch, seq, N_IN_COLS)[:, seq - keep:, COL_V:COL_V + KV_W]
    new_k = kn.astype(F32).reshape(batch, keep, N_KV, HEAD_DIM)
    new_v = v3.astype(F32).reshape(batch, keep, N_KV, HEAD_DIM)
    if k_cache is not None:
        new_k = jnp.concatenate([k_cache, new_k], axis=1)[:, -rows:]
        new_v = jnp.concatenate([v_cache, new_v], axis=1)[:, -rows:]

    x1, h2, mf, mi, cnt = _outproj(xt, ma, mb, w["w_out"], g1, w["norm2_g"], sc2, sh2,
                                   w["wr_hi"], w["wr_lo"], w["br"], cnt_in, per_token, tm_out, seq)
    return dict(batch=batch, seq=seq, t=t, per_token=per_token, tm_out=tm_out, x1=x1, h2=h2, mf=mf, mi=mi, cnt=cnt,
                g2=g2, state=(new_h.reshape(batch, D_MODEL), new_conv, new_k, new_v))


def _flat_dest(dest, tm):
    t = dest.shape[0]
    return dest[:, 0:2].reshape(t // tm, tm, 2).transpose(0, 2, 1).reshape(t // tm, 1, 2 * tm)


def _experts(groups, w, blk):
    total = sum(gr["t"] for gr in groups)
    n_blocks = -(-(2 * total) // blk) + N_EXPERTS
    nb_pad = -(-n_blocks // SUBLANES) * SUBLANES
    cnt = groups[-1]["cnt"]
    tmd = min(min(gr["t"] for gr in groups), 256)
    dest_d = []
    for gr in groups:
        dest, bmeta, pend = _plan(cnt, gr["mi"], blk, nb_pad)
        gr["dest_c"] = _flat_dest(dest, gr["tm_out"])
        dest_d.append(_flat_dest(dest, tmd))
    xs = _dispatch(pend[0, :N_EXPERTS], jnp.concatenate(dest_d, axis=0), [gr["h2"] for gr in groups],
                   n_blocks * blk, tmd, blk)
    ys = _moe(bmeta[:n_blocks, 0], bmeta[0:1, 1], xs, w["w_exp_up"], w["w_exp_down"], blk)
    return [_combine(gr["dest_c"], gr["x1"], gr["mf"], gr["g2"], ys, gr["per_token"], gr["tm_out"], gr["seq"])
            .reshape(gr["batch"], gr["seq"], D_MODEL) for gr in groups]


def kernel(x_prompt, x_sample, state_lru_h, state_lru_conv, cache_swa_k, cache_swa_v, c_prompt, c_sample,
           norm1_g, norm2_g, w_ada, b_ada, w_in, conv_w, conv_b, w_ra, b_ra, w_ri, b_ri, lru_lambda,
           q_norm_g, k_norm_g, attn_sink, w_out, w_router_g, b_router_g, w_router_e, b_router_e,
           w_exp_up, w_exp_down):
    assert norm1_g.shape[0] == 1, "single-layer trunk"
    bp, sp, _ = x_prompt.shape
    bs, ss, _ = x_sample.shape

    mod = _ada(jnp.concatenate([c_prompt, c_sample], axis=0), w_ada[0], b_ada[0])
    mods_p = jnp.split(mod[:bp], 6, axis=-1)
    mods_s = jnp.split(mod[bp:], 6, axis=-1)

    pad = LANES - N_EXP_GROUPS - N_EXPERTS
    wr = jnp.concatenate([w_router_g[0], w_router_e[0], jnp.zeros((D_MODEL, pad), F32)], axis=1)
    wr_hi = wr.astype(BF16)
    wr_lo = (wr - wr_hi.astype(F32)).astype(BF16)
    br = jnp.concatenate([b_router_g[0], b_router_e[0], jnp.zeros((pad,), F32)]).reshape(1, LANES)

    w = dict(norm1_g=norm1_g[0].reshape(1, D_MODEL), norm2_g=norm2_g[0].reshape(1, D_MODEL),
             w_in=w_in[0].astype(BF16), conv_w=conv_w[0], conv_b=conv_b[0], w_ra=w_ra[0], b_ra=b_ra[0],
             w_ri=w_ri[0], b_ri=b_ri[0], lru_lambda=lru_lambda[0], q_norm_g=q_norm_g[0], k_norm_g=k_norm_g[0],
             attn_sink=attn_sink[0], w_out=w_out[0].astype(BF16), wr_hi=wr_hi, wr_lo=wr_lo, br=br,
             w_exp_up=w_exp_up[0], w_exp_down=w_exp_down[0])

    prompt = _front(x_prompt, mods_p, jnp.arange(sp), jnp.zeros((bp, CONV_W - 1, D_MODEL), F32),
                    jnp.zeros((bp, D_MODEL), F32), None, None, w, jnp.zeros((1, LANES), F32), sp,
                    tm_in=min(1024, sp), tm_out=min(512, sp), nsub=2)
    sample = _front(x_sample, mods_s, PAST_LEN + jnp.arange(ss), state_lru_conv[0], state_lru_h[0],
                    cache_swa_k[0], cache_swa_v[0], w, prompt["cnt"], ss,
                    tm_in=bs * ss, tm_out=bs * ss, nsub=LRU_BLOCKS)
    yp, ys = _experts([prompt, sample], w, blk=256)
    ph, pc, pk, pv = prompt["state"]
    sh, sc, sk, sv = sample["state"]
    return (yp, ys, ph[None], pc[None], pk[None], pv[None], sh[None], sc[None], sk[None], sv[None])
```

```python
import functools
import math

import jax
import jax.numpy as jnp
from jax import lax
from jax.experimental import pallas as pl
from jax.experimental.pallas import tpu as pltpu

F32 = jnp.float32
BF16 = jnp.bfloat16
I32 = jnp.int32

D_MODEL = 2048
EPS = 1e-6
LANES = 128
SUBLANES = 8
CHUNK = 64
CONV_W = 4
LRU_BLOCKS = 16
LRU_BS = D_MODEL // LRU_BLOCKS
LRU_C = 8.0
HEAD_DIM = 128
N_HEADS = D_MODEL // HEAD_DIM
N_KV = 4
GROUP = N_HEADS // N_KV
WINDOW = 128
ROT_DIMS = HEAD_DIM // 4
ROPE_THETA = 500000.0
NEG = -1e30
N_EXP_GROUPS = 4
EXP_PER_GROUP = 8
N_EXPERTS = N_EXP_GROUPS * EXP_PER_GROUP
D_EXPERT = D_MODEL // 4
Q_W = N_HEADS * HEAD_DIM
KV_W = N_KV * HEAD_DIM
REF_COL_K = 2 * D_MODEL + Q_W
REF_COL_GA = REF_COL_K + 2 * KV_W
COL_XA, COL_YA, COL_Q = 0, D_MODEL, 2 * D_MODEL
COL_GA = COL_Q + Q_W
COL_GB = COL_GA + D_MODEL
COL_K = COL_GB + D_MODEL
COL_V = COL_K + KV_W
N_IN_COLS = COL_V + KV_W
PAST_LEN = 1024
ATT_SCALE = HEAD_DIM ** -0.5
ATTN_WAVE = 8
VMEM_LIMIT = 56 * 1024 * 1024


def _params(sem, vmem=VMEM_LIMIT):
    return pltpu.CompilerParams(dimension_semantics=sem, vmem_limit_bytes=vmem)


def _ada_body(c_ref, w_ref, b_ref, o_ref):
    c = c_ref[...]
    s = c * jax.nn.sigmoid(c)
    o_ref[...] = jnp.dot(s.astype(BF16), w_ref[...].astype(BF16), preferred_element_type=F32) + b_ref[...]


def _ada(c, w_ada, b_ada):
    bt, n, tn = c.shape[0], w_ada.shape[1], 1024
    return pl.pallas_call(
        _ada_body, grid=(n // tn,),
        in_specs=[pl.BlockSpec((bt, D_MODEL), lambda j: (0, 0)),
                  pl.BlockSpec((D_MODEL, tn), lambda j: (0, j)),
                  pl.BlockSpec((1, tn), lambda j: (0, j))],
        out_specs=pl.BlockSpec((bt, tn), lambda j: (0, j)),
        out_shape=jax.ShapeDtypeStruct((bt, n), F32),
        compiler_params=_params(("arbitrary",)), name="ada")(c, w_ada, b_ada.reshape(1, n))


def _norm_mod(x, g, sc, sh):
    ms = jnp.mean(x * x, axis=-1, keepdims=True)
    return (x * lax.rsqrt(ms + EPS) * g) * (1.0 + sc) + sh


def _inproj_body(x_ref, g_ref, sc_ref, sh_ref, w_ref, o_ref, h_scr):
    @pl.when(pl.program_id(1) == 0)
    def _():
        h_scr[...] = _norm_mod(x_ref[...], g_ref[...], sc_ref[...], sh_ref[...]).astype(BF16)

    o_ref[...] = jnp.dot(h_scr[...], w_ref[...], preferred_element_type=F32).astype(o_ref.dtype)


def _mod_spec(per_token, tm, tiles_per_batch):
    if per_token:
        return pl.BlockSpec((tm, D_MODEL), lambda i, *_: (i, 0))
    return pl.BlockSpec((None, 1, D_MODEL), lambda i, *_: (i // tiles_per_batch, 0, 0))


def _ref_col_tile(j, tn):
    n_head, n_gate, n_kv = REF_COL_K // tn, (COL_K - COL_GA) // tn, (2 * KV_W) // tn
    assert n_head * tn == REF_COL_K and n_gate * tn == COL_K - COL_GA and n_kv * tn == 2 * KV_W
    return jnp.where(j < n_head, j, jnp.where(j < n_head + n_gate, j + n_kv, j - n_gate))


def _inproj(x, norm_g, sc, sh, w_bf, per_token, tm, seq):
    t, tn = x.shape[0], 1024
    tpb = max(seq // tm, 1)
    return pl.pallas_call(
        _inproj_body, grid=(t // tm, N_IN_COLS // tn),
        in_specs=[pl.BlockSpec((tm, D_MODEL), lambda i, j: (i, 0)),
                  pl.BlockSpec((1, D_MODEL), lambda i, j: (0, 0)),
                  _mod_spec(per_token, tm, tpb), _mod_spec(per_token, tm, tpb),
                  pl.BlockSpec((D_MODEL, tn), lambda i, j: (0, _ref_col_tile(j, tn)))],
        out_specs=pl.BlockSpec((tm, tn), lambda i, j: (i, j)),
        out_shape=jax.ShapeDtypeStruct((t, N_IN_COLS), BF16),
        scratch_shapes=[pltpu.VMEM((tm, D_MODEL), BF16)],
        compiler_params=_params(("arbitrary", "arbitrary")), name="inproj")(x, norm_g, sc, sh, w_bf)


def _norm_rope_all(xs, gains, c, s1, s2):
    ms = [jnp.mean(x * x, axis=-1, keepdims=True) for x in xs]
    ys = [x * lax.rsqrt(m + EPS) * g for x, m, g in zip(xs, ms, gains)]
    up = [pltpu.roll(y, ROT_DIMS // 2, 1) for y in ys]
    dn = [pltpu.roll(y, HEAD_DIM - ROT_DIMS // 2, 1) for y in ys]
    return [(y * c + u * s1 + d * s2).astype(BF16) for y, u, d in zip(ys, up, dn)]


def _rope_tables(pos):
    half = ROT_DIMS // 2
    inv = ROPE_THETA ** (-2.0 * jnp.arange(half, dtype=F32) / ROT_DIMS)
    ang = pos.astype(F32)[:, None] * inv[None, :]
    cos, sin = jnp.cos(ang), jnp.sin(ang)
    n = pos.shape[0]
    rest = HEAD_DIM - ROT_DIMS
    c = jnp.concatenate([cos, cos, jnp.ones((n, rest), F32)], axis=1)
    s1 = jnp.concatenate([jnp.zeros((n, half), F32), sin, jnp.zeros((n, rest), F32)], axis=1)
    s2 = jnp.concatenate([-sin, jnp.zeros((n, half + rest), F32)], axis=1)
    return c, s1, s2


def _lru_body(xa_ref, ya_ref, ga_ref, cs_ref, h0_ref, cw_ref, cb_ref, wra_ref, bra_ref, wri_ref, bri_ref,
              lam_ref, o_ref, nh_ref, nc_ref, xpad, a_scr, b_scr, g_scr, *, seq, rc, nsub):
    pad = SUBLANES
    xpad[0:pad, :] = jnp.zeros((pad, nsub * LRU_BS), F32)
    xpad[pad - (CONV_W - 1):pad, :] = cs_ref[...]
    xpad[pad:pad + seq, :] = xa_ref[...].astype(F32)
    nc_ref[...] = xpad[seq + pad - (CONV_W - 1):seq + pad, :]

    z = -lam_ref[...]
    softplus = jnp.maximum(z, 0.0) + jnp.log1p(jnp.exp(-jnp.abs(z)))
    clam_all = -LRU_C * softplus
    for sb in range(nsub):
        ls = slice(sb * LRU_BS, (sb + 1) * LRU_BS)
        clam = clam_all[:, ls]
        wra = wra_ref[sb].astype(BF16)
        wri = wri_ref[sb].astype(BF16)
        for c0 in range(0, seq, rc):
            xc = cb_ref[:, ls]
            for tap in range(CONV_W):
                r0 = c0 + pad - (CONV_W - 1) + tap
                xc = xc + xpad[r0:r0 + rc, ls] * cw_ref[tap:tap + 1, ls]
            xcb = xc.astype(BF16)
            r = jax.nn.sigmoid(jnp.dot(xcb, wra, preferred_element_type=F32) + bra_ref[sb])
            ig = jax.nn.sigmoid(jnp.dot(xcb, wri, preferred_element_type=F32) + bri_ref[sb])
            log_a = r * clam
            a_scr[c0:c0 + rc, ls] = jnp.exp(log_a)
            th = jnp.tanh(log_a)
            mult = jnp.sqrt(jnp.maximum(-2.0 * th / (1.0 - th), 0.0))
            b_scr[c0:c0 + rc, ls] = mult * (ig * xc)
            ya = ya_ref[c0:c0 + rc, ls].astype(F32)
            ga = ga_ref[c0:c0 + rc, ls].astype(F32)
            g_scr[c0:c0 + rc, ls] = jax.nn.sigmoid(ga) * jax.nn.gelu(ya)

    rows = 2 * SUBLANES
    rid = lax.broadcasted_iota(I32, (SUBLANES, LRU_BS), 0)

    def step(it, carry):
        r0 = pl.multiple_of(it * rows, rows)
        out = []
        for sb in range(nsub):
            ls = slice(sb * LRU_BS, (sb + 1) * LRU_BS)
            c = carry[:, ls]
            hs = []
            for half in range(rows // SUBLANES):
                rh = pl.multiple_of(r0 + half * SUBLANES, SUBLANES)
                a = a_scr[pl.ds(rh, SUBLANES), ls]
                b = b_scr[pl.ds(rh, SUBLANES), ls]
                for d in (1, 2, 4):
                    keep = rid >= d
                    a_s = jnp.where(keep, pltpu.roll(a, d, 0), 1.0)
                    b_s = jnp.where(keep, pltpu.roll(b, d, 0), 0.0)
                    b = a * b_s + b
                    a = a * a_s
                hs.append(b + a * c)
                c = a[SUBLANES - 1:SUBLANES, :] * c + b[SUBLANES - 1:SUBLANES, :]
            out.append(c)
            h = jnp.concatenate(hs, axis=0)
            o_ref[pl.ds(r0, rows), ls] = (g_scr[pl.ds(r0, rows), ls] * h).astype(o_ref.dtype)
        return jnp.concatenate(out, axis=1) if nsub > 1 else out[0]

    nh_ref[...] = lax.fori_loop(0, seq // rows, step, h0_ref[...], unroll=min(2, seq // rows))


def _lru(proj, conv_state, h0, conv_w, conv_b, w_ra, b_ra, w_ri, b_ri, lam, batch, seq, nsub):
    t = proj.shape[0]
    rc = min(seq, 256)
    cw = nsub * LRU_BS
    col = lambda off: (lambda b, c: (b, off // cw + c))
    vec = lambda: pl.BlockSpec((1, cw), lambda b, c: (0, c))
    blk3 = lambda r: pl.BlockSpec((None, r, cw), lambda b, c: (b, 0, c))
    wblk = lambda: pl.BlockSpec((nsub, LRU_BS, LRU_BS), lambda b, c: (c, 0, 0))
    bblk = lambda: pl.BlockSpec((nsub, 1, LRU_BS), lambda b, c: (c, 0, 0))
    return pl.pallas_call(
        functools.partial(_lru_body, seq=seq, rc=rc, nsub=nsub), grid=(batch, LRU_BLOCKS // nsub),
        in_specs=[pl.BlockSpec((seq, cw), col(COL_XA)), pl.BlockSpec((seq, cw), col(COL_YA)),
                  pl.BlockSpec((seq, cw), col(COL_GA)), blk3(CONV_W - 1), blk3(1),
                  pl.BlockSpec((CONV_W, cw), lambda b, c: (0, c)), vec(), wblk(), bblk(), wblk(), bblk(), vec()],
        out_specs=[pl.BlockSpec((seq, cw), lambda b, c: (b, c)), blk3(1), blk3(CONV_W - 1)],
        out_shape=[jax.ShapeDtypeStruct((t, D_MODEL), BF16), jax.ShapeDtypeStruct((batch, 1, D_MODEL), F32),
                   jax.ShapeDtypeStruct((batch, CONV_W - 1, D_MODEL), F32)],
        scratch_shapes=[pltpu.VMEM((seq + SUBLANES, cw), F32), pltpu.VMEM((seq, cw), F32),
                        pltpu.VMEM((seq, cw), F32), pltpu.VMEM((seq, cw), F32)],
        compiler_params=_params(("arbitrary", "arbitrary")), name="lru")(
            proj, proj, proj, conv_state, h0.reshape(batch, 1, D_MODEL), conv_w, conv_b.reshape(1, D_MODEL),
            w_ra, b_ra.reshape(LRU_BLOCKS, 1, LRU_BS), w_ri, b_ri.reshape(LRU_BLOCKS, 1, LRU_BS),
            lam.reshape(1, D_MODEL))


def _attn_body(*refs, banded, qt, npv):
    if banded:
        (sink_ref, q_ref, kc_ref, vp_ref, vc_ref, gb_ref, ma_ref, qg_ref, kg_ref, c_ref, s1_ref, s2_ref,
         o_ref, kn_ref, kprev) = refs
    else:
        (sink_ref, q_ref, kp_ref, kc_ref, vp_ref, vc_ref, gb_ref, ma_ref, qg_ref, kg_ref, c_ref, s1_ref, s2_ref,
         o_ref, kn_ref) = refs
    nt = (((1,), (1,)), ((), ()))
    rows = qt
    merged_keys = banded
    i = pl.program_id(1)
    if banded:
        nk = npv + qt
        qc = (lax.broadcasted_iota(I32, (rows, nk), 0) % qt) // CHUNK
        col = lax.broadcasted_iota(I32, (rows, nk), 1)
        valid = ((col < npv) & (col // CHUNK >= qc) & (i > 0)) | ((col >= npv) & ((col - npv) // CHUNK <= qc))

        @pl.when(i == 0)
        def _():
            kprev[...] = jnp.zeros_like(kprev)

    tab = (c_ref[...], s1_ref[...], s2_ref[...])
    log2e = math.log2(math.e)
    slabs = ([q_ref[:, h * HEAD_DIM:(h + 1) * HEAD_DIM].astype(F32) for h in range(N_HEADS)]
             + [kc_ref[:, g * HEAD_DIM:(g + 1) * HEAD_DIM].astype(F32) for g in range(N_KV)])
    normed = _norm_rope_all(slabs, [qg_ref[...]] * N_HEADS + [kg_ref[...]] * N_KV, *tab)
    groups = []
    for g in range(N_KV):
        gs = slice(g * HEAD_DIM, (g + 1) * HEAD_DIM)
        kc = normed[N_HEADS + g]
        kn_ref[:, gs] = kc
        if banded:
            kp = kprev[:, gs]
            kprev[:, gs] = kc
        else:
            kp = kp_ref[:, gs].astype(BF16)
        vp, vc = vp_ref[:, gs].astype(BF16), vc_ref[:, gs].astype(BF16)
        if merged_keys:
            parts = [(jnp.concatenate([kp, kc], axis=0), jnp.concatenate([vp, vc], axis=0), valid)]
        else:
            parts = [(kp, vp, None), (kc, vc, None)]
        parts = [(k, jnp.concatenate([v, jnp.ones(v.shape, BF16)], axis=1), ok) for k, v, ok in parts]
        for h in range(g * GROUP, (g + 1) * GROUP):
            groups.append(dict(h=h, q=normed[h], sk=jnp.full((qt, 1), sink_ref[h] * log2e, F32), parts=parts))

    def logits2(q, k, ok):
        s = lax.dot_general(q, k, nt, preferred_element_type=F32) * (ATT_SCALE * log2e)
        return s if ok is None else jnp.where(ok, s, NEG)

    for wave in range(0, N_HEADS, ATTN_WAVE):
        active = groups[wave:wave + ATTN_WAVE]
        for gr in active:
            gr["s"] = [logits2(gr["q"], k, ok) for k, _, ok in gr["parts"]]
        for gr in active:
            m = gr["sk"]
            for s in gr["s"]:
                m = jnp.maximum(m, jnp.max(s, -1, keepdims=True))
            gr["m"] = m
        for gr in active:
            gr["p"] = [jnp.exp2(s - gr["m"]).astype(BF16) for s in gr["s"]]
        for gr in active:
            acc = jnp.dot(gr["p"][0], gr["parts"][0][1], preferred_element_type=F32)
            for p, (_, v, _) in zip(gr["p"][1:], gr["parts"][1:]):
                acc = acc + jnp.dot(p, v, preferred_element_type=F32)
            den = acc[:, HEAD_DIM:] + jnp.exp2(gr["sk"] - gr["m"])
            gr["o"] = acc[:, :HEAD_DIM] / den
        for gr in active:
            hs = slice(gr["h"] * HEAD_DIM, (gr["h"] + 1) * HEAD_DIM)
            gate = jax.nn.sigmoid(gb_ref[:, hs].astype(F32))
            o_ref[:, hs] = (ma_ref[:, hs].astype(F32) + gate * gr["o"]).astype(o_ref.dtype)


def _attn(sink, proj, ma, tabs, q_norm_g, k_norm_g, k_cache2, v_cache2, batch, seq, qt):
    t = proj.shape[0]
    nq = seq // qt
    banded = k_cache2 is None
    row = lambda width, col: pl.BlockSpec((qt, width), lambda b, i: (b * nq + i, col))
    vec = lambda: pl.BlockSpec((1, HEAD_DIM), lambda b, i: (0, 0))
    tab = lambda: pl.BlockSpec((qt, HEAD_DIM), lambda b, i: (i, 0))
    if banded:
        npv = qt
        prev_specs = [pl.BlockSpec((npv, KV_W), lambda b, i: (b * nq + jnp.maximum(i - 1, 0), COL_V // KV_W))]
        prev_args = [proj]
        scratch = [pltpu.VMEM((qt, KV_W), BF16)]
    else:
        npv = k_cache2.shape[0] // batch
        prev_specs = [pl.BlockSpec((npv, KV_W), lambda b, i: (b, 0))] * 2
        prev_args = [k_cache2, v_cache2]
        scratch = []
    in_specs = ([pl.BlockSpec(memory_space=pltpu.SMEM), row(Q_W, COL_Q // Q_W)]
                + prev_specs[:-1] + [row(KV_W, COL_K // KV_W), prev_specs[-1], row(KV_W, COL_V // KV_W),
                                     row(Q_W, COL_GB // Q_W), row(Q_W, 0), vec(), vec(), tab(), tab(), tab()])
    args = ([sink, proj] + prev_args[:-1] + [proj, prev_args[-1], proj, proj, ma,
                                             q_norm_g.reshape(1, HEAD_DIM), k_norm_g.reshape(1, HEAD_DIM), *tabs])
    return pl.pallas_call(
        functools.partial(_attn_body, banded=banded, qt=qt, npv=npv), grid=(batch, nq),
        in_specs=in_specs,
        out_specs=[pl.BlockSpec((qt, Q_W), lambda b, i: (b * nq + i, 0)),
                   pl.BlockSpec((None, qt, KV_W), lambda b, i: (b, 0, 0))],
        out_shape=[jax.ShapeDtypeStruct((t, Q_W), BF16), jax.ShapeDtypeStruct((batch, qt, KV_W), BF16)],
        scratch_shapes=scratch,
        compiler_params=_params(("arbitrary", "arbitrary")), name="attn")(*args)


def _lane_min(x):
    return jnp.min(x, axis=-1, keepdims=True)


def _outproj_body(x_ref, m_ref, w_ref, g1_ref, n2_ref, sc_ref, sh_ref, wrc_ref, wrh_ref, br_ref, cin_ref,
                  x1_ref, h2_ref, mf_ref, mi_ref, cnt_ref, base_scr, *, tm):
    @pl.when(pl.program_id(0) == 0)
    def _():
        base_scr[...] = cin_ref[...]

    x1 = x_ref[...] + g1_ref[...] * jnp.dot(m_ref[...], w_ref[...], preferred_element_type=F32)
    x1_ref[...] = x1
    h2 = _norm_mod(x1, n2_ref[...], sc_ref[...], sh_ref[...])
    h2_ref[...] = h2

    hi = h2.astype(BF16)
    lo = (h2 - hi.astype(F32)).astype(BF16)
    both = jnp.dot(hi, wrc_ref[...], preferred_element_type=F32)
    logits = (both[:, :LANES] + jnp.dot(lo, wrh_ref[...], preferred_element_type=F32) + both[:, LANES:]) + br_ref[...]
    lane = lax.broadcasted_iota(I32, (tm, LANES), 1).astype(F32)
    ninf = -jnp.inf
    big = float(LANES)
    is_g = lane < N_EXP_GROUPS
    lg = jnp.where(is_g, logits, ninf)
    mg = jnp.max(lg, -1, keepdims=True)
    gsel = _lane_min(jnp.where(lg == mg, lane, big))
    p_g = 1.0 / jnp.sum(jnp.where(is_g, jnp.exp(lg - mg), 0.0), -1, keepdims=True)
    lo_lane = N_EXP_GROUPS + EXP_PER_GROUP * gsel
    le = jnp.where((lane >= lo_lane) & (lane < lo_lane + EXP_PER_GROUP), logits, ninf)
    m1 = jnp.max(le, -1, keepdims=True)
    i1 = _lane_min(jnp.where(le == m1, lane, big))
    le2 = jnp.where(lane == i1, ninf, le)
    m2 = jnp.max(le2, -1, keepdims=True)
    i2 = _lane_min(jnp.where(le2 == m2, lane, big))
    e21 = jnp.exp(m2 - m1)
    w1 = p_g / (1.0 + e21)
    w2 = p_g * e21 / (1.0 + e21)
    e1, e2 = i1 - N_EXP_GROUPS, i2 - N_EXP_GROUPS

    onehot = ((lane == e1) | (lane == e2))
    row = lax.broadcasted_iota(I32, (tm, tm), 0)
    colm = lax.broadcasted_iota(I32, (tm, tm), 1)
    lower = jnp.where(row > colm, 1.0, 0.0).astype(BF16)
    before = jnp.dot(lower, jnp.where(onehot, 1.0, 0.0).astype(BF16), preferred_element_type=F32) + base_scr[...]
    r1 = jnp.sum(jnp.where(lane == e1, before, 0.0), -1, keepdims=True)
    r2 = jnp.sum(jnp.where(lane == e2, before, 0.0), -1, keepdims=True)
    base_scr[...] = base_scr[...] + jnp.sum(jnp.where(onehot, 1.0, 0.0), axis=0, keepdims=True)
    cnt_ref[...] = base_scr[...]

    mf_ref[...] = jnp.where(lane == 0.0, w1, jnp.where(lane == 1.0, w2, 0.0))
    mi = jnp.where(lane == 0.0, e1, jnp.where(lane == 1.0, e2, jnp.where(lane == 2.0, r1, jnp.where(lane == 3.0, r2, 0.0))))
    mi_ref[...] = mi.astype(I32)


def _outproj(x, merged, w_bf, g1, norm2_g, sc, sh, wr_cat, wr_hi, br, cnt_in, per_token, tm, seq):
    t = x.shape[0]
    tpb = max(seq // tm, 1)
    row = lambda: pl.BlockSpec((tm, D_MODEL), lambda i: (i, 0))
    full = lambda shape: pl.BlockSpec(shape, lambda i: (0, 0))
    meta = lambda: pl.BlockSpec((tm, LANES), lambda i: (i, 0))
    return pl.pallas_call(
        functools.partial(_outproj_body, tm=tm), grid=(t // tm,),
        in_specs=[row(), row(),
                  pl.BlockSpec((D_MODEL, D_MODEL), lambda i: (0, 0), pipeline_mode=pl.Buffered(1)),
                  _mod_spec(per_token, tm, tpb),
                  full((1, D_MODEL)), _mod_spec(per_token, tm, tpb), _mod_spec(per_token, tm, tpb),
                  full((D_MODEL, 2 * LANES)), full((D_MODEL, LANES)), full((1, LANES)), full((1, LANES))],
        out_specs=[row(), row(), meta(), meta(), full((1, LANES))],
        out_shape=[jax.ShapeDtypeStruct((t, D_MODEL), F32), jax.ShapeDtypeStruct((t, D_MODEL), F32),
                   jax.ShapeDtypeStruct((t, LANES), F32), jax.ShapeDtypeStruct((t, LANES), I32),
                   jax.ShapeDtypeStruct((1, LANES), F32)],
        scratch_shapes=[pltpu.VMEM((1, LANES), F32)],
        compiler_params=_params(("arbitrary",)), name="outproj")(
            x, merged, w_bf, g1, norm2_g, sc, sh, wr_cat, wr_hi, br, cnt_in)


def _lane_cumsum(x):
    lane = lax.broadcasted_iota(I32, x.shape, 1)
    for d in (1, 2, 4, 8, 16):
        x = x + jnp.where(lane >= d, pltpu.roll(x, d, 1), 0.0)
    return x


def _plan_body(cnt_ref, mi_ref, dest_ref, bexp_ref, pend_ref, *, tp, blk, nb_pad):
    cnt = jnp.broadcast_to(cnt_ref[...], (SUBLANES, LANES))
    lane8 = lax.broadcasted_iota(I32, (SUBLANES, LANES), 1)
    padded = jnp.where(lane8 < N_EXPERTS, jnp.ceil(cnt / blk) * blk, 0.0)
    pend = _lane_cumsum(padded)
    pstart = (pend - padded)[0:1, :]
    mi = mi_ref[...]
    lane = lax.broadcasted_iota(I32, (tp, LANES), 1)
    e1, e2, r1, r2 = mi[:, 0:1], mi[:, 1:2], mi[:, 2:3], mi[:, 3:4]
    d1 = jnp.sum(jnp.where(lane == e1, pstart, 0.0), -1, keepdims=True).astype(I32) + r1
    d2 = jnp.sum(jnp.where(lane == e2, pstart, 0.0), -1, keepdims=True).astype(I32) + r2
    dest_ref[...] = jnp.where(lane == 0, d1, jnp.where(lane == 1, d2, 0))

    @pl.when(pl.program_id(0) == 0)
    def _():
        jb = lax.broadcasted_iota(I32, (nb_pad, LANES), 0).astype(F32) * blk
        ln = lax.broadcasted_iota(I32, (nb_pad, LANES), 1)
        ends = jnp.where((ln < N_EXPERTS) & (pend[0:1, :] <= jb), 1.0, 0.0)
        be = jnp.minimum(jnp.sum(ends, -1, keepdims=True), N_EXPERTS - 1.0)
        used = pend[0:1, N_EXPERTS - 1:N_EXPERTS] / blk
        bexp_ref[...] = jnp.where(ln == 0, be, jnp.where(ln == 1, used, 0.0)).astype(I32)
        pend_ref[...] = pend.astype(I32)


def _plan(cnt, mi, blk, nb_pad):
    t = mi.shape[0]
    tp = min(t, 1024)
    return pl.pallas_call(
        functools.partial(_plan_body, tp=tp, blk=float(blk), nb_pad=nb_pad), grid=(t // tp,),
        in_specs=[pl.BlockSpec((1, LANES), lambda i: (0, 0)), pl.BlockSpec((tp, LANES), lambda i: (i, 0))],
        out_specs=[pl.BlockSpec((tp, LANES), lambda i: (i, 0)), pl.BlockSpec((nb_pad, LANES), lambda i: (0, 0)),
                   pl.BlockSpec((SUBLANES, LANES), lambda i: (0, 0))],
        out_shape=[jax.ShapeDtypeStruct((t, LANES), I32), jax.ShapeDtypeStruct((nb_pad, LANES), I32),
                   jax.ShapeDtypeStruct((SUBLANES, LANES), I32)],
        compiler_params=_params(("arbitrary",)), name="plan")(cnt, mi)


DISPATCH_SLOTS = 4


def _dispatch_body(*refs, tm, blk, bounds):
    pend_ref, dest_ref = refs[:2]
    h2_refs = refs[2:2 + len(bounds)]
    xs_ref, zbuf, tiles, lsem, rsem, zsem = refs[2 + len(bounds):]
    nsteps = bounds[-1][1]
    i = pl.program_id(0)

    def zero_copy(e):
        return pltpu.make_async_copy(zbuf, xs_ref.at[pl.ds(pl.multiple_of(pend_ref[e] - blk, blk), blk)], zsem)

    def has_rows(e):
        return pend_ref[e] > jnp.where(e > 0, pend_ref[jnp.maximum(e - 1, 0)], 0)

    @pl.when(i == 0)
    def _():
        zbuf[...] = jnp.zeros_like(zbuf)

        def zstart(e, c):
            @pl.when(has_rows(e))
            def _():
                zero_copy(e).start()
            return c

        def zwait(e, c):
            @pl.when(has_rows(e))
            def _():
                zero_copy(e).wait()
            return c

        lax.fori_loop(0, N_EXPERTS, zstart, 0)
        lax.fori_loop(0, N_EXPERTS, zwait, 0)

        def tail_copy(j):
            return pltpu.make_async_copy(zbuf, xs_ref.at[pl.ds(pl.multiple_of(j * blk, blk), blk)], zsem)

        def tstart(j, c):
            tail_copy(j).start()
            return c

        def twait(j, c):
            tail_copy(j).wait()
            return c

        used = pend_ref[N_EXPERTS - 1] // blk
        lax.fori_loop(used, xs_ref.shape[0] // blk, tstart, 0)
        lax.fori_loop(used, xs_ref.shape[0] // blk, twait, 0)

    def load_start(j):
        s = j % DISPATCH_SLOTS
        for (lo, hi), h2_ref in zip(bounds, h2_refs):
            @pl.when((j >= lo) & (j < hi))
            def _(lo=lo, h2_ref=h2_ref):
                pltpu.make_async_copy(h2_ref.at[pl.ds(pl.multiple_of((j - lo) * tm, tm), tm)], tiles.at[s],
                                      lsem.at[s]).start()

    def load_wait(j):
        s = j % DISPATCH_SLOTS
        pltpu.make_async_copy(h2_refs[0].at[pl.ds(0, tm)], tiles.at[s], lsem.at[s]).wait()

    def wait_rows(j):
        s = j % DISPATCH_SLOTS
        for _ in range(2):
            pltpu.make_async_copy(tiles.at[s], xs_ref.at[pl.ds(0, tm)], rsem.at[s]).wait()

    @pl.when(i == 0)
    def _():
        load_start(0)
        if nsteps > 1:
            load_start(1)

    @pl.when(i >= 2)
    def _():
        wait_rows(i - 2)

    @pl.when(i + 2 < nsteps)
    def _():
        load_start(i + 2)

    load_wait(i)
    slot = i % DISPATCH_SLOTS

    def start(r, c):
        for k in range(2):
            pltpu.make_async_copy(tiles.at[slot, pl.ds(r, 1)], xs_ref.at[pl.ds(dest_ref[0, k * tm + r], 1)],
                                  rsem.at[slot]).start(priority=k)
        return c

    lax.fori_loop(0, tm, start, 0, unroll=8)

    @pl.when(i == nsteps - 1)
    def _():
        if nsteps > 1:
            wait_rows(i - 1)
        wait_rows(i)


def _dispatch(pend, dest3, h2_list, n_rows, tm, blk):
    bounds, lo = [], 0
    for h2 in h2_list:
        bounds.append((lo, lo + h2.shape[0] // tm))
        lo = bounds[-1][1]
    return pl.pallas_call(
        functools.partial(_dispatch_body, tm=tm, blk=blk, bounds=tuple(bounds)), grid=(lo,),
        in_specs=[pl.BlockSpec(memory_space=pltpu.SMEM),
                  pl.BlockSpec((None, 1, 2 * tm), lambda i: (i, 0, 0), memory_space=pltpu.SMEM)]
                 + [pl.BlockSpec(memory_space=pl.ANY)] * len(h2_list),
        out_specs=pl.BlockSpec(memory_space=pl.ANY),
        out_shape=jax.ShapeDtypeStruct((n_rows, D_MODEL), F32),
        scratch_shapes=[pltpu.VMEM((blk, D_MODEL), F32), pltpu.VMEM((DISPATCH_SLOTS, tm, D_MODEL), F32),
                        pltpu.SemaphoreType.DMA((DISPATCH_SLOTS,)), pltpu.SemaphoreType.DMA((DISPATCH_SLOTS,)),
                        pltpu.SemaphoreType.DMA(())],
        compiler_params=_params(("arbitrary",)), name="dispatch")(pend, dest3, *h2_list)


def _moe_body(bexp_ref, used_ref, xs_ref, wu_ref, wd_ref, ys_ref, wu_bf, wd_bf):
    j = pl.program_id(0)
    changed = (j == 0) | (bexp_ref[j] != bexp_ref[jnp.maximum(j - 1, 0)])

    @pl.when(changed)
    def _():
        wu_bf[...] = wu_ref[...].astype(BF16)
        wd_bf[...] = wd_ref[...].astype(BF16)

    @pl.when(j < used_ref[0])
    def _():
        up = jnp.dot(xs_ref[...].astype(BF16), wu_bf[...], preferred_element_type=F32)
        act = jax.nn.silu(up[:, :D_EXPERT]) * up[:, D_EXPERT:]
        ys_ref[...] = jnp.dot(act.astype(BF16), wd_bf[...], preferred_element_type=F32)

    @pl.when(j >= used_ref[0])
    def _():
        ys_ref[...] = jnp.zeros_like(ys_ref)


def _moe(bexp, used, xs, w_up, w_down, blk):
    p = xs.shape[0]
    grid_spec = pltpu.PrefetchScalarGridSpec(
        num_scalar_prefetch=2, grid=(p // blk,),
        in_specs=[pl.BlockSpec((blk, D_MODEL), lambda j, be, us: (j, 0)),
                  pl.BlockSpec((None, D_MODEL, 2 * D_EXPERT), lambda j, be, us: (be[j], 0, 0)),
                  pl.BlockSpec((None, D_EXPERT, D_MODEL), lambda j, be, us: (be[j], 0, 0))],
        out_specs=pl.BlockSpec((blk, D_MODEL), lambda j, be, us: (j, 0)),
        scratch_shapes=[pltpu.VMEM((D_MODEL, 2 * D_EXPERT), BF16), pltpu.VMEM((D_EXPERT, D_MODEL), BF16)])
    return pl.pallas_call(
        _moe_body, grid_spec=grid_spec, out_shape=jax.ShapeDtypeStruct((p, D_MODEL), F32),
        compiler_params=_params(("arbitrary",)), name="moe")(bexp, used, xs, w_up, w_down)


def _combine_body(dest_ref, dnext_ref, x1_ref, mf_ref, g2_ref, ys_ref, o_ref, ybuf, sem, *, tm, nsteps):
    i = pl.program_id(0)

    def issue(dref, s):
        def start(r, c):
            for k in range(2):
                pltpu.make_async_copy(ys_ref.at[pl.ds(dref[0, k * tm + r], 1)], ybuf.at[s, k, pl.ds(r, 1)],
                                      sem.at[s]).start(priority=k)
            return c

        lax.fori_loop(0, tm, start, 0, unroll=8)

    @pl.when(i == 0)
    def _():
        issue(dest_ref, 0)

    for s in range(2):
        @pl.when(i % 2 == s)
        def _(s=s):
            @pl.when(i + 1 < nsteps)
            def _():
                issue(dnext_ref, 1 - s)

            for k in range(2):
                pltpu.make_async_copy(ys_ref.at[pl.ds(0, tm)], ybuf.at[s, k], sem.at[s]).wait()
            mf = mf_ref[...]
            moe = mf[:, 0:1] * ybuf[s, 0] + mf[:, 1:2] * ybuf[s, 1]
            o_ref[...] = x1_ref[...] + g2_ref[...] * moe


def _combine(dest3, x1, mf, g2, ys, per_token, tm, seq):
    t = x1.shape[0]
    tpb = max(seq // tm, 1)
    nsteps = t // tm
    return pl.pallas_call(
        functools.partial(_combine_body, tm=tm, nsteps=nsteps), grid=(nsteps,),
        in_specs=[pl.BlockSpec((None, 1, 2 * tm), lambda i: (i, 0, 0), memory_space=pltpu.SMEM),
                  pl.BlockSpec((None, 1, 2 * tm), lambda i: (jnp.minimum(i + 1, nsteps - 1), 0, 0),
                               memory_space=pltpu.SMEM),
                  pl.BlockSpec((tm, D_MODEL), lambda i: (i, 0)),
                  pl.BlockSpec((tm, LANES), lambda i: (i, 0)),
                  _mod_spec(per_token, tm, tpb),
                  pl.BlockSpec(memory_space=pl.ANY)],
        out_specs=pl.BlockSpec((tm, D_MODEL), lambda i: (i, 0)),
        out_shape=jax.ShapeDtypeStruct((t, D_MODEL), F32),
        scratch_shapes=[pltpu.VMEM((2, 2, tm, D_MODEL), F32), pltpu.SemaphoreType.DMA((2,))],
        compiler_params=_params(("arbitrary",)), name="combine")(dest3, dest3, x1, mf, g2, ys)


def _group_mod(mod6, per_token, seq):
    if per_token:
        return jnp.repeat(mod6, seq, axis=0)
    return mod6[:, None, :]


def _front(x, mods, pos, conv_state, h0, k_cache, v_cache, w, cnt_in, seq, tm_in, tm_out, nsub):
    batch = x.shape[0]
    t = batch * seq
    xt = x.reshape(t, D_MODEL)
    per_token = seq < tm_out
    sh1, sc1, g1, sh2, sc2, g2 = [_group_mod(m, per_token, seq) for m in mods]

    proj = _inproj(xt, w["norm1_g"], sc1, sh1, w["w_in"], per_token, tm_in, seq)

    ma, new_h, new_conv = _lru(proj, conv_state, h0, w["conv_w"], w["conv_b"], w["w_ra"], w["b_ra"],
                               w["w_ri"], w["b_ri"], w["lru_lambda"], batch, seq, nsub)

    tabs = _rope_tables(pos)
    keep = min(WINDOW, seq)
    if k_cache is None:
        merged, kn = _attn(w["attn_sink"], proj, ma, tabs, w["q_norm_g"], w["k_norm_g"], None, None, batch, seq, keep)
    else:
        rows = k_cache.shape[1]
        merged, kn = _attn(w["attn_sink"], proj, ma, tabs, w["q_norm_g"], w["k_norm_g"],
                           k_cache.reshape(batch * rows, KV_W), v_cache.reshape(batch * rows, KV_W), batch, seq, keep)
    v3 = proj.reshape(batch, seq, N_IN_COLS)[:, seq - keep:, COL_V:COL_V + KV_W]
    new_k = kn.astype(F32).reshape(batch, keep, N_KV, HEAD_DIM)
    new_v = v3.astype(F32).reshape(batch, keep, N_KV, HEAD_DIM)
    if k_cache is not None:
        new_k = jnp.concatenate([k_cache, new_k], axis=1)[:, -rows:]
        new_v = jnp.concatenate([v_cache, new_v], axis=1)[:, -rows:]

    x1, h2, mf, mi, cnt = _outproj(xt, merged, w["w_out"], g1, w["norm2_g"], sc2, sh2,
                                   w["wr_cat"], w["wr_hi"], w["br"], cnt_in, per_token, tm_out, seq)
    return dict(batch=batch, seq=seq, t=t, per_token=per_token, tm_out=tm_out, x1=x1, h2=h2, mf=mf, mi=mi, cnt=cnt,
                g2=g2, state=(new_h.reshape(batch, D_MODEL), new_conv, new_k, new_v))


def _flat_dest(dest, tm):
    t = dest.shape[0]
    return dest[:, 0:2].reshape(t // tm, tm, 2).transpose(0, 2, 1).reshape(t // tm, 1, 2 * tm)


def _experts(groups, w, blk):
    total = sum(gr["t"] for gr in groups)
    n_blocks = -(-(2 * total) // blk) + N_EXPERTS
    nb_pad = -(-n_blocks // SUBLANES) * SUBLANES
    cnt = groups[-1]["cnt"]
    tmd = min(min(gr["t"] for gr in groups), 256)
    dest_d = []
    for gr in groups:
        dest, bmeta, pend = _plan(cnt, gr["mi"], blk, nb_pad)
        gr["dest_c"] = _flat_dest(dest, gr["tm_out"])
        dest_d.append(_flat_dest(dest, tmd))
    xs = _dispatch(pend[0, :N_EXPERTS], jnp.concatenate(dest_d, axis=0), [gr["h2"] for gr in groups],
                   n_blocks * blk, tmd, blk)
    ys = _moe(bmeta[:n_blocks, 0], bmeta[0:1, 1], xs, w["w_exp_up"], w["w_exp_down"], blk)
    return [_combine(gr["dest_c"], gr["x1"], gr["mf"], gr["g2"], ys, gr["per_token"], gr["tm_out"], gr["seq"])
            .reshape(gr["batch"], gr["seq"], D_MODEL) for gr in groups]


def kernel(x_prompt, x_sample, state_lru_h, state_lru_conv, cache_swa_k, cache_swa_v, c_prompt, c_sample,
           norm1_g, norm2_g, w_ada, b_ada, w_in, conv_w, conv_b, w_ra, b_ra, w_ri, b_ri, lru_lambda,
           q_norm_g, k_norm_g, attn_sink, w_out, w_router_g, b_router_g, w_router_e, b_router_e,
           w_exp_up, w_exp_down):
    assert norm1_g.shape[0] == 1, "single-layer trunk"
    bp, sp, _ = x_prompt.shape
    bs, ss, _ = x_sample.shape

    mod = _ada(jnp.concatenate([c_prompt, c_sample], axis=0), w_ada[0], b_ada[0])
    mods_p = jnp.split(mod[:bp], 6, axis=-1)
    mods_s = jnp.split(mod[bp:], 6, axis=-1)

    pad = LANES - N_EXP_GROUPS - N_EXPERTS
    wr = jnp.concatenate([w_router_g[0], w_router_e[0], jnp.zeros((D_MODEL, pad), F32)], axis=1)
    wr_hi = wr.astype(BF16)
    wr_lo = (wr - wr_hi.astype(F32)).astype(BF16)
    br = jnp.concatenate([b_router_g[0], b_router_e[0], jnp.zeros((pad,), F32)]).reshape(1, LANES)

    w = dict(norm1_g=norm1_g[0].reshape(1, D_MODEL), norm2_g=norm2_g[0].reshape(1, D_MODEL),
             w_in=w_in[0].astype(BF16), conv_w=conv_w[0], conv_b=conv_b[0], w_ra=w_ra[0], b_ra=b_ra[0],
             w_ri=w_ri[0], b_ri=b_ri[0], lru_lambda=lru_lambda[0], q_norm_g=q_norm_g[0], k_norm_g=k_norm_g[0],
             attn_sink=attn_sink[0], w_out=w_out[0].astype(BF16), wr_hi=wr_hi,
             wr_cat=jnp.concatenate([wr_hi, wr_lo], axis=1), br=br,
             w_exp_up=w_exp_up[0], w_exp_down=w_exp_down[0])

    prompt = _front(x_prompt, mods_p, jnp.arange(sp), jnp.zeros((bp, CONV_W - 1, D_MODEL), F32),
                    jnp.zeros((bp, D_MODEL), F32), None, None, w, jnp.zeros((1, LANES), F32), sp,
                    tm_in=min(1024, sp), tm_out=min(512, sp), nsub=2)
    sample = _front(x_sample, mods_s, PAST_LEN + jnp.arange(ss), state_lru_conv[0], state_lru_h[0],
                    cache_swa_k[0], cache_swa_v[0], w, prompt["cnt"], ss,
                    tm_in=bs * ss, tm_out=bs * ss, nsub=LRU_BLOCKS)
    yp, ys = _experts([prompt, sample], w, blk=256)
    ph, pc, pk, pv = prompt["state"]
    sh, sc, sk, sv = sample["state"]
    return (yp, ys, ph[None], pc[None], pk[None], pv[None], sh[None], sc[None], sk[None], sv[None])
```

```python
import functools
import math

import jax
import jax.numpy as jnp
from jax import lax
from jax.experimental import pallas as pl
from jax.experimental.pallas import tpu as pltpu

F32 = jnp.float32
BF16 = jnp.bfloat16
I32 = jnp.int32

D_MODEL = 2048
EPS = 1e-6
LANES = 128
SUBLANES = 8
CHUNK = 64
CONV_W = 4
LRU_BLOCKS = 16
LRU_BS = D_MODEL // LRU_BLOCKS
LRU_C = 8.0
HEAD_DIM = 128
N_HEADS = D_MODEL // HEAD_DIM
N_KV = 4
GROUP = N_HEADS // N_KV
WINDOW = 128
ROT_DIMS = HEAD_DIM // 4
ROPE_THETA = 500000.0
NEG = -1e30
N_EXP_GROUPS = 4
EXP_PER_GROUP = 8
N_EXPERTS = N_EXP_GROUPS * EXP_PER_GROUP
D_EXPERT = D_MODEL // 4
Q_W = N_HEADS * HEAD_DIM
KV_W = N_KV * HEAD_DIM
REF_COL_K = 2 * D_MODEL + Q_W
REF_COL_GA = REF_COL_K + 2 * KV_W
COL_XA, COL_YA, COL_Q = 0, D_MODEL, 2 * D_MODEL
COL_GA = COL_Q + Q_W
COL_GB = COL_GA + D_MODEL
COL_K = COL_GB + D_MODEL
COL_V = COL_K + KV_W
N_IN_COLS = COL_V + KV_W
PAST_LEN = 1024
ATT_SCALE = HEAD_DIM ** -0.5
ATTN_WAVE = 8
VMEM_LIMIT = 56 * 1024 * 1024


def _params(sem, vmem=VMEM_LIMIT):
    return pltpu.CompilerParams(dimension_semantics=sem, vmem_limit_bytes=vmem)


def _ada_body(c_ref, w_ref, b_ref, o_ref):
    c = c_ref[...]
    s = c * jax.nn.sigmoid(c)
    o_ref[...] = jnp.dot(s.astype(BF16), w_ref[...].astype(BF16), preferred_element_type=F32) + b_ref[...]


def _ada(c, w_ada, b_ada):
    bt, n, tn = c.shape[0], w_ada.shape[1], 1024
    return pl.pallas_call(
        _ada_body, grid=(n // tn,),
        in_specs=[pl.BlockSpec((bt, D_MODEL), lambda j: (0, 0)),
                  pl.BlockSpec((D_MODEL, tn), lambda j: (0, j)),
                  pl.BlockSpec((1, tn), lambda j: (0, j))],
        out_specs=pl.BlockSpec((bt, tn), lambda j: (0, j)),
        out_shape=jax.ShapeDtypeStruct((bt, n), F32),
        compiler_params=_params(("arbitrary",)), name="ada")(c, w_ada, b_ada.reshape(1, n))


def _norm_mod(x, g, sc, sh):
    ms = jnp.mean(x * x, axis=-1, keepdims=True)
    return (x * lax.rsqrt(ms + EPS) * g) * (1.0 + sc) + sh


def _inproj_body(x_ref, g_ref, sc_ref, sh_ref, w_ref, o_ref, h_scr):
    @pl.when(pl.program_id(1) == 0)
    def _():
        h_scr[...] = _norm_mod(x_ref[...], g_ref[...], sc_ref[...], sh_ref[...]).astype(BF16)

    o_ref[...] = jnp.dot(h_scr[...], w_ref[...], preferred_element_type=F32).astype(o_ref.dtype)


def _mod_spec(per_token, tm, tiles_per_batch, tile=lambda i: i):
    if per_token:
        return pl.BlockSpec((tm, D_MODEL), lambda i, *_: (tile(i), 0))
    return pl.BlockSpec((None, 1, D_MODEL), lambda i, *_: (tile(i) // tiles_per_batch, 0, 0))


def _ref_col_tile(j, tn):
    n_head, n_gate, n_kv = REF_COL_K // tn, (COL_K - COL_GA) // tn, (2 * KV_W) // tn
    assert n_head * tn == REF_COL_K and n_gate * tn == COL_K - COL_GA and n_kv * tn == 2 * KV_W
    return jnp.where(j < n_head, j, jnp.where(j < n_head + n_gate, j + n_kv, j - n_gate))


def _inproj(x, norm_g, sc, sh, w_bf, per_token, tm, seq):
    t, tn = x.shape[0], 1024
    tpb = max(seq // tm, 1)
    return pl.pallas_call(
        _inproj_body, grid=(t // tm, N_IN_COLS // tn),
        in_specs=[pl.BlockSpec((tm, D_MODEL), lambda i, j: (i, 0)),
                  pl.BlockSpec((1, D_MODEL), lambda i, j: (0, 0)),
                  _mod_spec(per_token, tm, tpb), _mod_spec(per_token, tm, tpb),
                  pl.BlockSpec((D_MODEL, tn), lambda i, j: (0, _ref_col_tile(j, tn)))],
        out_specs=pl.BlockSpec((tm, tn), lambda i, j: (i, j)),
        out_shape=jax.ShapeDtypeStruct((t, N_IN_COLS), BF16),
        scratch_shapes=[pltpu.VMEM((tm, D_MODEL), BF16)],
        compiler_params=_params(("arbitrary", "arbitrary")), name="inproj")(x, norm_g, sc, sh, w_bf)


def _norm_rope_all(xs, gains, c, s1, s2):
    ms = [jnp.mean(x * x, axis=-1, keepdims=True) for x in xs]
    ys = [x * lax.rsqrt(m + EPS) * g for x, m, g in zip(xs, ms, gains)]
    up = [pltpu.roll(y, ROT_DIMS // 2, 1) for y in ys]
    dn = [pltpu.roll(y, HEAD_DIM - ROT_DIMS // 2, 1) for y in ys]
    return [(y * c + u * s1 + d * s2).astype(BF16) for y, u, d in zip(ys, up, dn)]


def _rope_tables(pos):
    half = ROT_DIMS // 2
    inv = ROPE_THETA ** (-2.0 * jnp.arange(half, dtype=F32) / ROT_DIMS)
    ang = pos.astype(F32)[:, None] * inv[None, :]
    cos, sin = jnp.cos(ang), jnp.sin(ang)
    n = pos.shape[0]
    rest = HEAD_DIM - ROT_DIMS
    c = jnp.concatenate([cos, cos, jnp.ones((n, rest), F32)], axis=1)
    s1 = jnp.concatenate([jnp.zeros((n, half), F32), sin, jnp.zeros((n, rest), F32)], axis=1)
    s2 = jnp.concatenate([-sin, jnp.zeros((n, half + rest), F32)], axis=1)
    return c, s1, s2


def _lru_body(xa_ref, ya_ref, ga_ref, cs_ref, h0_ref, cw_ref, cb_ref, wra_ref, bra_ref, wri_ref, bri_ref,
              lam_ref, o_ref, nh_ref, nc_ref, xpad, a_scr, b_scr, g_scr, *, seq, rc, nsub):
    pad = SUBLANES
    xpad[0:pad, :] = jnp.zeros((pad, nsub * LRU_BS), F32)
    xpad[pad - (CONV_W - 1):pad, :] = cs_ref[...]
    xpad[pad:pad + seq, :] = xa_ref[...].astype(F32)
    nc_ref[...] = xpad[seq + pad - (CONV_W - 1):seq + pad, :]

    z = -lam_ref[...]
    softplus = jnp.maximum(z, 0.0) + jnp.log1p(jnp.exp(-jnp.abs(z)))
    clam_all = -LRU_C * softplus
    for sb in range(nsub):
        ls = slice(sb * LRU_BS, (sb + 1) * LRU_BS)
        clam = clam_all[:, ls]
        wra = wra_ref[sb].astype(BF16)
        wri = wri_ref[sb].astype(BF16)
        for c0 in range(0, seq, rc):
            xc = cb_ref[:, ls]
            for tap in range(CONV_W):
                r0 = c0 + pad - (CONV_W - 1) + tap
                xc = xc + xpad[r0:r0 + rc, ls] * cw_ref[tap:tap + 1, ls]
            xcb = xc.astype(BF16)
            r = jax.nn.sigmoid(jnp.dot(xcb, wra, preferred_element_type=F32) + bra_ref[sb])
            ig = jax.nn.sigmoid(jnp.dot(xcb, wri, preferred_element_type=F32) + bri_ref[sb])
            log_a = r * clam
            a_scr[c0:c0 + rc, ls] = jnp.exp(log_a)
            th = jnp.tanh(log_a)
            mult = jnp.sqrt(jnp.maximum(-2.0 * th / (1.0 - th), 0.0))
            b_scr[c0:c0 + rc, ls] = mult * (ig * xc)
            ya = ya_ref[c0:c0 + rc, ls].astype(F32)
            ga = ga_ref[c0:c0 + rc, ls].astype(F32)
            g_scr[c0:c0 + rc, ls] = jax.nn.sigmoid(ga) * jax.nn.gelu(ya)

    rows = 2 * SUBLANES
    rid = lax.broadcasted_iota(I32, (SUBLANES, LRU_BS), 0)

    def step(it, carry):
        r0 = pl.multiple_of(it * rows, rows)
        out = []
        for sb in range(nsub):
            ls = slice(sb * LRU_BS, (sb + 1) * LRU_BS)
            c = carry[:, ls]
            hs = []
            for half in range(rows // SUBLANES):
                rh = pl.multiple_of(r0 + half * SUBLANES, SUBLANES)
                a = a_scr[pl.ds(rh, SUBLANES), ls]
                b = b_scr[pl.ds(rh, SUBLANES), ls]
                for d in (1, 2, 4):
                    keep = rid >= d
                    a_s = jnp.where(keep, pltpu.roll(a, d, 0), 1.0)
                    b_s = jnp.where(keep, pltpu.roll(b, d, 0), 0.0)
                    b = a * b_s + b
                    a = a * a_s
                hs.append(b + a * c)
                c = a[SUBLANES - 1:SUBLANES, :] * c + b[SUBLANES - 1:SUBLANES, :]
            out.append(c)
            h = jnp.concatenate(hs, axis=0)
            o_ref[pl.ds(r0, rows), ls] = (g_scr[pl.ds(r0, rows), ls] * h).astype(o_ref.dtype)
        return jnp.concatenate(out, axis=1) if nsub > 1 else out[0]

    nh_ref[...] = lax.fori_loop(0, seq // rows, step, h0_ref[...], unroll=min(2, seq // rows))


def _lru(proj, conv_state, h0, conv_w, conv_b, w_ra, b_ra, w_ri, b_ri, lam, batch, seq, nsub):
    t = proj.shape[0]
    rc = min(seq, 256)
    cw = nsub * LRU_BS
    col = lambda off: (lambda b, c: (b, off // cw + c))
    vec = lambda: pl.BlockSpec((1, cw), lambda b, c: (0, c))
    blk3 = lambda r: pl.BlockSpec((None, r, cw), lambda b, c: (b, 0, c))
    wblk = lambda: pl.BlockSpec((nsub, LRU_BS, LRU_BS), lambda b, c: (c, 0, 0))
    bblk = lambda: pl.BlockSpec((nsub, 1, LRU_BS), lambda b, c: (c, 0, 0))
    return pl.pallas_call(
        functools.partial(_lru_body, seq=seq, rc=rc, nsub=nsub), grid=(batch, LRU_BLOCKS // nsub),
        in_specs=[pl.BlockSpec((seq, cw), col(COL_XA)), pl.BlockSpec((seq, cw), col(COL_YA)),
                  pl.BlockSpec((seq, cw), col(COL_GA)), blk3(CONV_W - 1), blk3(1),
                  pl.BlockSpec((CONV_W, cw), lambda b, c: (0, c)), vec(), wblk(), bblk(), wblk(), bblk(), vec()],
        out_specs=[pl.BlockSpec((seq, cw), lambda b, c: (b, c)), blk3(1), blk3(CONV_W - 1)],
        out_shape=[jax.ShapeDtypeStruct((t, D_MODEL), BF16), jax.ShapeDtypeStruct((batch, 1, D_MODEL), F32),
                   jax.ShapeDtypeStruct((batch, CONV_W - 1, D_MODEL), F32)],
        scratch_shapes=[pltpu.VMEM((seq + SUBLANES, cw), F32), pltpu.VMEM((seq, cw), F32),
                        pltpu.VMEM((seq, cw), F32), pltpu.VMEM((seq, cw), F32)],
        compiler_params=_params(("arbitrary", "arbitrary")), name="lru")(
            proj, proj, proj, conv_state, h0.reshape(batch, 1, D_MODEL), conv_w, conv_b.reshape(1, D_MODEL),
            w_ra, b_ra.reshape(LRU_BLOCKS, 1, LRU_BS), w_ri, b_ri.reshape(LRU_BLOCKS, 1, LRU_BS),
            lam.reshape(1, D_MODEL))


def _attn_body(*refs, banded, qt, npv, subs):
    if banded:
        (sink_ref, q_ref, kc_ref, vp_ref, vc_ref, gb_ref, ma_ref, qg_ref, kg_ref, c_ref, s1_ref, s2_ref,
         o_ref, kn_ref, kprev) = refs
    else:
        (sink_ref, q_ref, kp_ref, kc_ref, vp_ref, vc_ref, gb_ref, ma_ref, qg_ref, kg_ref, c_ref, s1_ref, s2_ref,
         o_ref, kn_ref) = refs
    nt = (((1,), (1,)), ((), ()))
    i = pl.program_id(1)
    log2e = math.log2(math.e)
    if banded:
        nk = npv + qt
        qc = lax.broadcasted_iota(I32, (qt, nk), 0) // CHUNK
        col = lax.broadcasted_iota(I32, (qt, nk), 1)
        band = ((col < npv) & (col // CHUNK >= qc)) | ((col >= npv) & ((col - npv) // CHUNK <= qc))

        @pl.when(i == 0)
        def _():
            kprev[...] = jnp.zeros_like(kprev)

    def logits2(q, k, ok):
        s = lax.dot_general(q, k, nt, preferred_element_type=F32) * (ATT_SCALE * log2e)
        return s if ok is None else jnp.where(ok, s, NEG)

    prev_k = None
    for sub in range(subs):
        rs = slice(sub * qt, (sub + 1) * qt)
        tab = (c_ref[rs, :], s1_ref[rs, :], s2_ref[rs, :])
        slabs = ([q_ref[rs, h * HEAD_DIM:(h + 1) * HEAD_DIM].astype(F32) for h in range(N_HEADS)]
                 + [kc_ref[rs, g * HEAD_DIM:(g + 1) * HEAD_DIM].astype(F32) for g in range(N_KV)])
        normed = _norm_rope_all(slabs, [qg_ref[...]] * N_HEADS + [kg_ref[...]] * N_KV, *tab)
        if banded:
            valid = band if sub > 0 else band & ((col >= npv) | (i > 0))
        units = []
        for g in range(N_KV):
            gs = slice(g * HEAD_DIM, (g + 1) * HEAD_DIM)
            kc = normed[N_HEADS + g]
            vc = vc_ref[rs, gs].astype(BF16)
            if sub > 0:
                kp, vp = prev_k[g], vc_ref[(sub - 1) * qt:sub * qt, gs].astype(BF16)
            elif banded:
                kp, vp = kprev[:, gs], vp_ref[:, gs].astype(BF16)
            else:
                kp, vp = kp_ref[:, gs].astype(BF16), vp_ref[:, gs].astype(BF16)
            if sub == subs - 1:
                kn_ref[:, gs] = kc
                if banded:
                    kprev[:, gs] = kc
            if banded:
                parts = [(jnp.concatenate([kp, kc], axis=0), jnp.concatenate([vp, vc], axis=0), valid)]
            else:
                parts = [(kp, vp, None), (kc, vc, None)]
            parts = [(k, jnp.concatenate([v, jnp.ones(v.shape, BF16)], axis=1), ok) for k, v, ok in parts]
            for h in range(g * GROUP, (g + 1) * GROUP):
                units.append(dict(h=h, q=normed[h], sk=jnp.full((qt, 1), sink_ref[h] * log2e, F32), parts=parts))
        prev_k = normed[N_HEADS:]

        for wave in range(0, N_HEADS, ATTN_WAVE):
            active = units[wave:wave + ATTN_WAVE]
            for u in active:
                u["s"] = [logits2(u["q"], k, ok) for k, _, ok in u["parts"]]
            for u in active:
                m = u["sk"]
                for s in u["s"]:
                    m = jnp.maximum(m, jnp.max(s, -1, keepdims=True))
                u["m"] = m
            for u in active:
                u["p"] = [jnp.exp2(s - u["m"]).astype(BF16) for s in u["s"]]
            for u in active:
                acc = jnp.dot(u["p"][0], u["parts"][0][1], preferred_element_type=F32)
                for p, (_, v, _) in zip(u["p"][1:], u["parts"][1:]):
                    acc = acc + jnp.dot(p, v, preferred_element_type=F32)
                den = acc[:, HEAD_DIM:] + jnp.exp2(u["sk"] - u["m"])
                u["o"] = acc[:, :HEAD_DIM] / den
            for u in active:
                hs = slice(u["h"] * HEAD_DIM, (u["h"] + 1) * HEAD_DIM)
                gate = jax.nn.sigmoid(gb_ref[rs, hs].astype(F32))
                o_ref[rs, hs] = (ma_ref[rs, hs].astype(F32) + gate * u["o"]).astype(o_ref.dtype)


def _attn(sink, proj, ma, tabs, q_norm_g, k_norm_g, k_cache2, v_cache2, batch, seq, qt, subs):
    t = proj.shape[0]
    nq = seq // qt
    ns = nq // subs
    banded = k_cache2 is None
    row = lambda width, col: pl.BlockSpec((subs * qt, width), lambda b, i: (b * ns + i, col))
    vec = lambda: pl.BlockSpec((1, HEAD_DIM), lambda b, i: (0, 0))
    tab = lambda: pl.BlockSpec((subs * qt, HEAD_DIM), lambda b, i: (i, 0))
    if banded:
        npv = qt
        prev_specs = [pl.BlockSpec((npv, KV_W), lambda b, i: (b * nq + jnp.maximum(i * subs - 1, 0), COL_V // KV_W))]
        prev_args = [proj]
        scratch = [pltpu.VMEM((qt, KV_W), BF16)]
    else:
        npv = k_cache2.shape[0] // batch
        prev_specs = [pl.BlockSpec((npv, KV_W), lambda b, i: (b, 0))] * 2
        prev_args = [k_cache2, v_cache2]
        scratch = []
    in_specs = ([pl.BlockSpec(memory_space=pltpu.SMEM), row(Q_W, COL_Q // Q_W)]
                + prev_specs[:-1] + [row(KV_W, COL_K // KV_W), prev_specs[-1], row(KV_W, COL_V // KV_W),
                                     row(Q_W, COL_GB // Q_W), row(Q_W, 0), vec(), vec(), tab(), tab(), tab()])
    args = ([sink, proj] + prev_args[:-1] + [proj, prev_args[-1], proj, proj, ma,
                                             q_norm_g.reshape(1, HEAD_DIM), k_norm_g.reshape(1, HEAD_DIM), *tabs])
    return pl.pallas_call(
        functools.partial(_attn_body, banded=banded, qt=qt, npv=npv, subs=subs), grid=(batch, ns),
        in_specs=in_specs,
        out_specs=[pl.BlockSpec((subs * qt, Q_W), lambda b, i: (b * ns + i, 0)),
                   pl.BlockSpec((None, qt, KV_W), lambda b, i: (b, 0, 0))],
        out_shape=[jax.ShapeDtypeStruct((t, Q_W), BF16), jax.ShapeDtypeStruct((batch, qt, KV_W), BF16)],
        scratch_shapes=scratch,
        compiler_params=_params(("arbitrary", "arbitrary")), name="attn")(*args)


def _lane_min(x):
    return jnp.min(x, axis=-1, keepdims=True)


OUTPROJ_CHUNKS = 4


def _outproj_body(x_ref, m_ref, w_ref, g1_ref, n2_ref, sc_ref, sh_ref, wrc_ref, wrh_ref, br_ref, cin_ref,
                  x1_ref, h2_ref, mf_ref, mi_ref, cnt_ref, base_scr, y_even, y_odd, *, tm):
    i = pl.program_id(0)

    @pl.when(i == 0)
    def _():
        base_scr[...] = cin_ref[...]
        y_odd[...] = jnp.zeros_like(y_odd)

    cw = D_MODEL // OUTPROJ_CHUNKS
    for parity, (y_new, y_old) in enumerate(((y_even, y_odd), (y_odd, y_even))):
        @pl.when(i % 2 == parity)
        def _(y_new=y_new, y_old=y_old):
            def product(c):
                y_new[:, c * cw:(c + 1) * cw] = jnp.dot(m_ref[...], w_ref[:, c * cw:(c + 1) * cw],
                                                        preferred_element_type=F32)

            _outproj_finish(y_old, i > 0, product, x_ref, g1_ref, n2_ref, sc_ref, sh_ref, wrc_ref, wrh_ref, br_ref,
                            x1_ref, h2_ref, mf_ref, mi_ref, cnt_ref, base_scr, tm)


def _outproj_finish(y_ref, live, product, x_ref, g1_ref, n2_ref, sc_ref, sh_ref, wrc_ref, wrh_ref, br_ref,
                    x1_ref, h2_ref, mf_ref, mi_ref, cnt_ref, base_scr, tm):
    product(0)
    x1 = x_ref[...] + g1_ref[...] * y_ref[...]
    x1_ref[...] = x1
    h2 = _norm_mod(x1, n2_ref[...], sc_ref[...], sh_ref[...])
    h2_ref[...] = h2
    product(1)

    hi = h2.astype(BF16)
    lo = (h2 - hi.astype(F32)).astype(BF16)
    both = jnp.dot(hi, wrc_ref[...], preferred_element_type=F32)
    logits = (both[:, :LANES] + jnp.dot(lo, wrh_ref[...], preferred_element_type=F32) + both[:, LANES:]) + br_ref[...]
    lane = lax.broadcasted_iota(I32, (tm, LANES), 1).astype(F32)
    ninf = -jnp.inf
    big = float(LANES)
    is_g = lane < N_EXP_GROUPS
    lg = jnp.where(is_g, logits, ninf)
    mg = jnp.max(lg, -1, keepdims=True)
    gsel = _lane_min(jnp.where(lg == mg, lane, big))
    p_g = 1.0 / jnp.sum(jnp.where(is_g, jnp.exp(lg - mg), 0.0), -1, keepdims=True)
    lo_lane = N_EXP_GROUPS + EXP_PER_GROUP * gsel
    le = jnp.where((lane >= lo_lane) & (lane < lo_lane + EXP_PER_GROUP), logits, ninf)
    m1 = jnp.max(le, -1, keepdims=True)
    i1 = _lane_min(jnp.where(le == m1, lane, big))
    le2 = jnp.where(lane == i1, ninf, le)
    m2 = jnp.max(le2, -1, keepdims=True)
    i2 = _lane_min(jnp.where(le2 == m2, lane, big))
    e21 = jnp.exp(m2 - m1)
    w1 = p_g / (1.0 + e21)
    w2 = p_g * e21 / (1.0 + e21)
    e1, e2 = i1 - N_EXP_GROUPS, i2 - N_EXP_GROUPS
    product(2)

    onehot = ((lane == e1) | (lane == e2))
    row = lax.broadcasted_iota(I32, (tm, tm), 0)
    colm = lax.broadcasted_iota(I32, (tm, tm), 1)
    lower = jnp.where(row > colm, 1.0, 0.0).astype(BF16)
    before = jnp.dot(lower, jnp.where(onehot, 1.0, 0.0).astype(BF16), preferred_element_type=F32) + base_scr[...]
    r1 = jnp.sum(jnp.where(lane == e1, before, 0.0), -1, keepdims=True)
    r2 = jnp.sum(jnp.where(lane == e2, before, 0.0), -1, keepdims=True)
    base_scr[...] = base_scr[...] + jnp.sum(jnp.where(onehot & live, 1.0, 0.0), axis=0, keepdims=True)
    cnt_ref[...] = base_scr[...]

    mf_ref[...] = jnp.where(lane == 0.0, w1, jnp.where(lane == 1.0, w2, 0.0))
    mi = jnp.where(lane == 0.0, e1, jnp.where(lane == 1.0, e2, jnp.where(lane == 2.0, r1, jnp.where(lane == 3.0, r2, 0.0))))
    mi_ref[...] = mi.astype(I32)
    product(3)


def _outproj(x, merged, w_bf, g1, norm2_g, sc, sh, wr_cat, wr_hi, br, cnt_in, per_token, tm, seq):
    t = x.shape[0]
    tpb = max(seq // tm, 1)
    n = t // tm
    done = lambda i: jnp.maximum(i - 1, 0)
    row = lambda: pl.BlockSpec((tm, D_MODEL), lambda i: (done(i), 0))
    full = lambda shape: pl.BlockSpec(shape, lambda i: (0, 0))
    meta = lambda: pl.BlockSpec((tm, LANES), lambda i: (done(i), 0))
    mod = lambda: _mod_spec(per_token, tm, tpb, done)
    return pl.pallas_call(
        functools.partial(_outproj_body, tm=tm), grid=(n + 1,),
        in_specs=[row(), pl.BlockSpec((tm, D_MODEL), lambda i: (jnp.minimum(i, n - 1), 0)),
                  pl.BlockSpec((D_MODEL, D_MODEL), lambda i: (0, 0), pipeline_mode=pl.Buffered(1)),
                  mod(), full((1, D_MODEL)), mod(), mod(),
                  full((D_MODEL, 2 * LANES)), full((D_MODEL, LANES)), full((1, LANES)), full((1, LANES))],
        out_specs=[row(), row(), meta(), meta(), full((1, LANES))],
        out_shape=[jax.ShapeDtypeStruct((t, D_MODEL), F32), jax.ShapeDtypeStruct((t, D_MODEL), F32),
                   jax.ShapeDtypeStruct((t, LANES), F32), jax.ShapeDtypeStruct((t, LANES), I32),
                   jax.ShapeDtypeStruct((1, LANES), F32)],
        scratch_shapes=[pltpu.VMEM((1, LANES), F32), pltpu.VMEM((tm, D_MODEL), F32), pltpu.VMEM((tm, D_MODEL), F32)],
        compiler_params=_params(("arbitrary",)), name="outproj")(
            x, merged, w_bf, g1, norm2_g, sc, sh, wr_cat, wr_hi, br, cnt_in)


def _lane_cumsum(x):
    lane = lax.broadcasted_iota(I32, x.shape, 1)
    for d in (1, 2, 4, 8, 16):
        x = x + jnp.where(lane >= d, pltpu.roll(x, d, 1), 0.0)
    return x


def _plan_body(cnt_ref, mi_ref, dest_ref, bexp_ref, pend_ref, *, tp, blk, nb_pad):
    cnt = jnp.broadcast_to(cnt_ref[...], (SUBLANES, LANES))
    lane8 = lax.broadcasted_iota(I32, (SUBLANES, LANES), 1)
    padded = jnp.where(lane8 < N_EXPERTS, jnp.ceil(cnt / blk) * blk, 0.0)
    pend = _lane_cumsum(padded)
    pstart = (pend - padded)[0:1, :]
    mi = mi_ref[...]
    lane = lax.broadcasted_iota(I32, (tp, LANES), 1)
    e1, e2, r1, r2 = mi[:, 0:1], mi[:, 1:2], mi[:, 2:3], mi[:, 3:4]
    d1 = jnp.sum(jnp.where(lane == e1, pstart, 0.0), -1, keepdims=True).astype(I32) + r1
    d2 = jnp.sum(jnp.where(lane == e2, pstart, 0.0), -1, keepdims=True).astype(I32) + r2
    dest_ref[...] = jnp.where(lane == 0, d1, jnp.where(lane == 1, d2, 0))

    @pl.when(pl.program_id(0) == 0)
    def _():
        jb = lax.broadcasted_iota(I32, (nb_pad, LANES), 0).astype(F32) * blk
        ln = lax.broadcasted_iota(I32, (nb_pad, LANES), 1)
        ends = jnp.where((ln < N_EXPERTS) & (pend[0:1, :] <= jb), 1.0, 0.0)
        be = jnp.minimum(jnp.sum(ends, -1, keepdims=True), N_EXPERTS - 1.0)
        used = pend[0:1, N_EXPERTS - 1:N_EXPERTS] / blk
        bexp_ref[...] = jnp.where(ln == 0, be, jnp.where(ln == 1, used, 0.0)).astype(I32)
        pend_ref[...] = pend.astype(I32)


def _plan(cnt, mi, blk, nb_pad):
    t = mi.shape[0]
    tp = min(t, 4096)
    return pl.pallas_call(
        functools.partial(_plan_body, tp=tp, blk=float(blk), nb_pad=nb_pad), grid=(t // tp,),
        in_specs=[pl.BlockSpec((1, LANES), lambda i: (0, 0)), pl.BlockSpec((tp, LANES), lambda i: (i, 0))],
        out_specs=[pl.BlockSpec((tp, LANES), lambda i: (i, 0)), pl.BlockSpec((nb_pad, LANES), lambda i: (0, 0)),
                   pl.BlockSpec((SUBLANES, LANES), lambda i: (0, 0))],
        out_shape=[jax.ShapeDtypeStruct((t, LANES), I32), jax.ShapeDtypeStruct((nb_pad, LANES), I32),
                   jax.ShapeDtypeStruct((SUBLANES, LANES), I32)],
        compiler_params=_params(("arbitrary",)), name="plan")(cnt, mi)


DISPATCH_SLOTS = 4


def _dispatch_body(*refs, tm, blk, bounds):
    pend_ref, dest_ref = refs[:2]
    h2_refs = refs[2:2 + len(bounds)]
    xs_ref, zbuf, tiles, lsem, rsem, zsem = refs[2 + len(bounds):]
    nsteps = bounds[-1][1]
    i = pl.program_id(0)

    def zero_copy(e):
        return pltpu.make_async_copy(zbuf, xs_ref.at[pl.ds(pl.multiple_of(pend_ref[e] - blk, blk), blk)], zsem)

    def has_rows(e):
        return pend_ref[e] > jnp.where(e > 0, pend_ref[jnp.maximum(e - 1, 0)], 0)

    @pl.when(i == 0)
    def _():
        zbuf[...] = jnp.zeros_like(zbuf)

        def zstart(e, c):
            @pl.when(has_rows(e))
            def _():
                zero_copy(e).start()
            return c

        def zwait(e, c):
            @pl.when(has_rows(e))
            def _():
                zero_copy(e).wait()
            return c

        lax.fori_loop(0, N_EXPERTS, zstart, 0)
        lax.fori_loop(0, N_EXPERTS, zwait, 0)

        def tail_copy(j):
            return pltpu.make_async_copy(zbuf, xs_ref.at[pl.ds(pl.multiple_of(j * blk, blk), blk)], zsem)

        def tstart(j, c):
            tail_copy(j).start()
            return c

        def twait(j, c):
            tail_copy(j).wait()
            return c

        used = pend_ref[N_EXPERTS - 1] // blk
        lax.fori_loop(used, xs_ref.shape[0] // blk, tstart, 0)
        lax.fori_loop(used, xs_ref.shape[0] // blk, twait, 0)

    def load_start(j):
        s = j % DISPATCH_SLOTS
        for (lo, hi), h2_ref in zip(bounds, h2_refs):
            @pl.when((j >= lo) & (j < hi))
            def _(lo=lo, h2_ref=h2_ref):
                pltpu.make_async_copy(h2_ref.at[pl.ds(pl.multiple_of((j - lo) * tm, tm), tm)], tiles.at[s],
                                      lsem.at[s]).start()

    def load_wait(j):
        s = j % DISPATCH_SLOTS
        pltpu.make_async_copy(h2_refs[0].at[pl.ds(0, tm)], tiles.at[s], lsem.at[s]).wait()

    def wait_rows(j):
        s = j % DISPATCH_SLOTS
        for _ in range(2):
            pltpu.make_async_copy(tiles.at[s], xs_ref.at[pl.ds(0, tm)], rsem.at[s]).wait()

    @pl.when(i == 0)
    def _():
        load_start(0)
        if nsteps > 1:
            load_start(1)

    @pl.when(i >= 2)
    def _():
        wait_rows(i - 2)

    @pl.when(i + 2 < nsteps)
    def _():
        load_start(i + 2)

    load_wait(i)
    slot = i % DISPATCH_SLOTS

    def start(r, c):
        for k in range(2):
            pltpu.make_async_copy(tiles.at[slot, pl.ds(r, 1)], xs_ref.at[pl.ds(dest_ref[0, k * tm + r], 1)],
                                  rsem.at[slot]).start(priority=k)
        return c

    lax.fori_loop(0, tm, start, 0, unroll=8)

    @pl.when(i == nsteps - 1)
    def _():
        if nsteps > 1:
            wait_rows(i - 1)
        wait_rows(i)


def _dispatch(pend, dest3, h2_list, n_rows, tm, blk):
    bounds, lo = [], 0
    for h2 in h2_list:
        bounds.append((lo, lo + h2.shape[0] // tm))
        lo = bounds[-1][1]
    return pl.pallas_call(
        functools.partial(_dispatch_body, tm=tm, blk=blk, bounds=tuple(bounds)), grid=(lo,),
        in_specs=[pl.BlockSpec(memory_space=pltpu.SMEM),
                  pl.BlockSpec((None, 1, 2 * tm), lambda i: (i, 0, 0), memory_space=pltpu.SMEM)]
                 + [pl.BlockSpec(memory_space=pl.ANY)] * len(h2_list),
        out_specs=pl.BlockSpec(memory_space=pl.ANY),
        out_shape=jax.ShapeDtypeStruct((n_rows, D_MODEL), F32),
        scratch_shapes=[pltpu.VMEM((blk, D_MODEL), F32), pltpu.VMEM((DISPATCH_SLOTS, tm, D_MODEL), F32),
                        pltpu.SemaphoreType.DMA((DISPATCH_SLOTS,)), pltpu.SemaphoreType.DMA((DISPATCH_SLOTS,)),
                        pltpu.SemaphoreType.DMA(())],
        compiler_params=_params(("arbitrary",)), name="dispatch")(pend, dest3, *h2_list)


def _moe_body(bexp_ref, used_ref, xs_ref, wu_ref, wd_ref, ys_ref, wu_bf, wd_bf):
    j = pl.program_id(0)
    changed = (j == 0) | (bexp_ref[j] != bexp_ref[jnp.maximum(j - 1, 0)])

    @pl.when(changed)
    def _():
        wu_bf[...] = wu_ref[...].astype(BF16)
        wd_bf[...] = wd_ref[...].astype(BF16)

    @pl.when(j < used_ref[0])
    def _():
        up = jnp.dot(xs_ref[...].astype(BF16), wu_bf[...], preferred_element_type=F32)
        act = jax.nn.silu(up[:, :D_EXPERT]) * up[:, D_EXPERT:]
        ys_ref[...] = jnp.dot(act.astype(BF16), wd_bf[...], preferred_element_type=F32)

    @pl.when(j >= used_ref[0])
    def _():
        ys_ref[...] = jnp.zeros_like(ys_ref)


def _moe(bexp, used, xs, w_up, w_down, blk):
    p = xs.shape[0]
    grid_spec = pltpu.PrefetchScalarGridSpec(
        num_scalar_prefetch=2, grid=(p // blk,),
        in_specs=[pl.BlockSpec((blk, D_MODEL), lambda j, be, us: (j, 0)),
                  pl.BlockSpec((None, D_MODEL, 2 * D_EXPERT), lambda j, be, us: (be[j], 0, 0)),
                  pl.BlockSpec((None, D_EXPERT, D_MODEL), lambda j, be, us: (be[j], 0, 0))],
        out_specs=pl.BlockSpec((blk, D_MODEL), lambda j, be, us: (j, 0)),
        scratch_shapes=[pltpu.VMEM((D_MODEL, 2 * D_EXPERT), BF16), pltpu.VMEM((D_EXPERT, D_MODEL), BF16)])
    return pl.pallas_call(
        _moe_body, grid_spec=grid_spec, out_shape=jax.ShapeDtypeStruct((p, D_MODEL), F32),
        compiler_params=_params(("arbitrary",)), name="moe")(bexp, used, xs, w_up, w_down)


def _combine_body(dest_ref, dnext_ref, x1_ref, mf_ref, g2_ref, ys_ref, o_ref, ybuf, sem, *, tm, nsteps):
    i = pl.program_id(0)

    def issue(dref, s):
        def start(r, c):
            for k in range(2):
                pltpu.make_async_copy(ys_ref.at[pl.ds(dref[0, k * tm + r], 1)], ybuf.at[s, k, pl.ds(r, 1)],
                                      sem.at[s]).start(priority=k)
            return c

        lax.fori_loop(0, tm, start, 0, unroll=8)

    @pl.when(i == 0)
    def _():
        issue(dest_ref, 0)

    for s in range(2):
        @pl.when(i % 2 == s)
        def _(s=s):
            @pl.when(i + 1 < nsteps)
            def _():
                issue(dnext_ref, 1 - s)

            for k in range(2):
                pltpu.make_async_copy(ys_ref.at[pl.ds(0, tm)], ybuf.at[s, k], sem.at[s]).wait()
            mf = mf_ref[...]
            moe = mf[:, 0:1] * ybuf[s, 0] + mf[:, 1:2] * ybuf[s, 1]
            o_ref[...] = x1_ref[...] + g2_ref[...] * moe


def _combine(dest3, x1, mf, g2, ys, per_token, tm, seq):
    t = x1.shape[0]
    tpb = max(seq // tm, 1)
    nsteps = t // tm
    return pl.pallas_call(
        functools.partial(_combine_body, tm=tm, nsteps=nsteps), grid=(nsteps,),
        in_specs=[pl.BlockSpec((None, 1, 2 * tm), lambda i: (i, 0, 0), memory_space=pltpu.SMEM),
                  pl.BlockSpec((None, 1, 2 * tm), lambda i: (jnp.minimum(i + 1, nsteps - 1), 0, 0),
                               memory_space=pltpu.SMEM),
                  pl.BlockSpec((tm, D_MODEL), lambda i: (i, 0)),
                  pl.BlockSpec((tm, LANES), lambda i: (i, 0)),
                  _mod_spec(per_token, tm, tpb),
                  pl.BlockSpec(memory_space=pl.ANY)],
        out_specs=pl.BlockSpec((tm, D_MODEL), lambda i: (i, 0)),
        out_shape=jax.ShapeDtypeStruct((t, D_MODEL), F32),
        scratch_shapes=[pltpu.VMEM((2, 2, tm, D_MODEL), F32), pltpu.SemaphoreType.DMA((2,))],
        compiler_params=_params(("arbitrary",)), name="combine")(dest3, dest3, x1, mf, g2, ys)


def _group_mod(mod6, per_token, seq):
    if per_token:
        return jnp.repeat(mod6, seq, axis=0)
    return mod6[:, None, :]


def _front(x, mods, pos, conv_state, h0, k_cache, v_cache, w, cnt_in, seq, tm_in, tm_out, nsub):
    batch = x.shape[0]
    t = batch * seq
    xt = x.reshape(t, D_MODEL)
    per_token = seq < tm_out
    sh1, sc1, g1, sh2, sc2, g2 = [_group_mod(m, per_token, seq) for m in mods]

    proj = _inproj(xt, w["norm1_g"], sc1, sh1, w["w_in"], per_token, tm_in, seq)

    ma, new_h, new_conv = _lru(proj, conv_state, h0, w["conv_w"], w["conv_b"], w["w_ra"], w["b_ra"],
                               w["w_ri"], w["b_ri"], w["lru_lambda"], batch, seq, nsub)

    tabs = _rope_tables(pos)
    keep = min(WINDOW, seq)
    if k_cache is None:
        merged, kn = _attn(w["attn_sink"], proj, ma, tabs, w["q_norm_g"], w["k_norm_g"], None, None, batch, seq, keep,
                           subs=2 if seq % (2 * keep) == 0 else 1)
    else:
        rows = k_cache.shape[1]
        merged, kn = _attn(w["attn_sink"], proj, ma, tabs, w["q_norm_g"], w["k_norm_g"],
                           k_cache.reshape(batch * rows, KV_W), v_cache.reshape(batch * rows, KV_W), batch, seq, keep,
                           subs=1)
    v3 = proj.reshape(batch, seq, N_IN_COLS)[:, seq - keep:, COL_V:COL_V + KV_W]
    new_k = kn.astype(F32).reshape(batch, keep, N_KV, HEAD_DIM)
    new_v = v3.astype(F32).reshape(batch, keep, N_KV, HEAD_DIM)
    if k_cache is not None:
        new_k = jnp.concatenate([k_cache, new_k], axis=1)[:, -rows:]
        new_v = jnp.concatenate([v_cache, new_v], axis=1)[:, -rows:]

    x1, h2, mf, mi, cnt = _outproj(xt, merged, w["w_out"], g1, w["norm2_g"], sc2, sh2,
                                   w["wr_cat"], w["wr_hi"], w["br"], cnt_in, per_token, tm_out, seq)
    return dict(batch=batch, seq=seq, t=t, per_token=per_token, tm_out=tm_out, x1=x1, h2=h2, mf=mf, mi=mi, cnt=cnt,
                g2=g2, state=(new_h.reshape(batch, D_MODEL), new_conv, new_k, new_v))


def _flat_dest(dest, tm):
    t = dest.shape[0]
    return dest[:, 0:2].reshape(t // tm, tm, 2).transpose(0, 2, 1).reshape(t // tm, 1, 2 * tm)


def _experts(groups, w, blk):
    total = sum(gr["t"] for gr in groups)
    n_blocks = -(-(2 * total) // blk) + N_EXPERTS
    nb_pad = -(-n_blocks // SUBLANES) * SUBLANES
    cnt = groups[-1]["cnt"]
    tmd = min(min(gr["t"] for gr in groups), 256)
    dest_d = []
    for gr in groups:
        dest, bmeta, pend = _plan(cnt, gr["mi"], blk, nb_pad)
        gr["dest_c"] = _flat_dest(dest, gr["tm_out"])
        dest_d.append(_flat_dest(dest, tmd))
    xs = _dispatch(pend[0, :N_EXPERTS], jnp.concatenate(dest_d, axis=0), [gr["h2"] for gr in groups],
                   n_blocks * blk, tmd, blk)
    ys = _moe(bmeta[:n_blocks, 0], bmeta[0:1, 1], xs, w["w_exp_up"], w["w_exp_down"], blk)
    return [_combine(gr["dest_c"], gr["x1"], gr["mf"], gr["g2"], ys, gr["per_token"], gr["tm_out"], gr["seq"])
            .reshape(gr["batch"], gr["seq"], D_MODEL) for gr in groups]


def kernel(x_prompt, x_sample, state_lru_h, state_lru_conv, cache_swa_k, cache_swa_v, c_prompt, c_sample,
           norm1_g, norm2_g, w_ada, b_ada, w_in, conv_w, conv_b, w_ra, b_ra, w_ri, b_ri, lru_lambda,
           q_norm_g, k_norm_g, attn_sink, w_out, w_router_g, b_router_g, w_router_e, b_router_e,
           w_exp_up, w_exp_down):
    assert norm1_g.shape[0] == 1, "single-layer trunk"
    bp, sp, _ = x_prompt.shape
    bs, ss, _ = x_sample.shape

    mod = _ada(jnp.concatenate([c_prompt, c_sample], axis=0), w_ada[0], b_ada[0])
    mods_p = jnp.split(mod[:bp], 6, axis=-1)
    mods_s = jnp.split(mod[bp:], 6, axis=-1)

    pad = LANES - N_EXP_GROUPS - N_EXPERTS
    wr = jnp.concatenate([w_router_g[0], w_router_e[0], jnp.zeros((D_MODEL, pad), F32)], axis=1)
    wr_hi = wr.astype(BF16)
    wr_lo = (wr - wr_hi.astype(F32)).astype(BF16)
    br = jnp.concatenate([b_router_g[0], b_router_e[0], jnp.zeros((pad,), F32)]).reshape(1, LANES)

    w = dict(norm1_g=norm1_g[0].reshape(1, D_MODEL), norm2_g=norm2_g[0].reshape(1, D_MODEL),
             w_in=w_in[0].astype(BF16), conv_w=conv_w[0], conv_b=conv_b[0], w_ra=w_ra[0], b_ra=b_ra[0],
             w_ri=w_ri[0], b_ri=b_ri[0], lru_lambda=lru_lambda[0], q_norm_g=q_norm_g[0], k_norm_g=k_norm_g[0],
             attn_sink=attn_sink[0], w_out=w_out[0].astype(BF16), wr_hi=wr_hi,
             wr_cat=jnp.concatenate([wr_hi, wr_lo], axis=1), br=br,
             w_exp_up=w_exp_up[0], w_exp_down=w_exp_down[0])

    prompt = _front(x_prompt, mods_p, jnp.arange(sp), jnp.zeros((bp, CONV_W - 1, D_MODEL), F32),
                    jnp.zeros((bp, D_MODEL), F32), None, None, w, jnp.zeros((1, LANES), F32), sp,
                    tm_in=min(1024, sp), tm_out=min(512, sp), nsub=4)
    sample = _front(x_sample, mods_s, PAST_LEN + jnp.arange(ss), state_lru_conv[0], state_lru_h[0],
                    cache_swa_k[0], cache_swa_v[0], w, prompt["cnt"], ss,
                    tm_in=bs * ss, tm_out=bs * ss, nsub=LRU_BLOCKS)
    yp, ys = _experts([prompt, sample], w, blk=256)
    ph, pc, pk, pv = prompt["state"]
    sh, sc, sk, sv = sample["state"]
    return (yp, ys, ph[None], pc[None], pk[None], pv[None], sh[None], sc[None], sk[None], sv[None])
```

```python
import functools
import math

import jax
import jax.numpy as jnp
from jax import lax
from jax.experimental import pallas as pl
from jax.experimental.pallas import tpu as pltpu

F32 = jnp.float32
BF16 = jnp.bfloat16
I32 = jnp.int32

D_MODEL = 2048
EPS = 1e-6
LANES = 128
SUBLANES = 8
CHUNK = 64
CONV_W = 4
LRU_BLOCKS = 16
LRU_BS = D_MODEL // LRU_BLOCKS
LRU_C = 8.0
HEAD_DIM = 128
N_HEADS = D_MODEL // HEAD_DIM
N_KV = 4
GROUP = N_HEADS // N_KV
WINDOW = 128
ROT_DIMS = HEAD_DIM // 4
ROPE_THETA = 500000.0
NEG = -1e30
N_EXP_GROUPS = 4
EXP_PER_GROUP = 8
N_EXPERTS = N_EXP_GROUPS * EXP_PER_GROUP
D_EXPERT = D_MODEL // 4
Q_W = N_HEADS * HEAD_DIM
KV_W = N_KV * HEAD_DIM
REF_COL_K = 2 * D_MODEL + Q_W
REF_COL_GA = REF_COL_K + 2 * KV_W
COL_XA, COL_YA, COL_Q = 0, D_MODEL, 2 * D_MODEL
COL_GA = COL_Q + Q_W
COL_GB = COL_GA + D_MODEL
COL_K = COL_GB + D_MODEL
COL_V = COL_K + KV_W
N_IN_COLS = COL_V + KV_W
PAST_LEN = 1024
ATT_SCALE = HEAD_DIM ** -0.5
ATTN_WAVE = 8
VMEM_LIMIT = 56 * 1024 * 1024


def _params(sem, vmem=VMEM_LIMIT):
    return pltpu.CompilerParams(dimension_semantics=sem, vmem_limit_bytes=vmem)


def _ada_body(c_ref, w_ref, b_ref, o_ref):
    c = c_ref[...]
    s = c * jax.nn.sigmoid(c)
    o_ref[...] = jnp.dot(s.astype(BF16), w_ref[...].astype(BF16), preferred_element_type=F32) + b_ref[...]


def _ada(c, w_ada, b_ada):
    bt, n, tn = c.shape[0], w_ada.shape[1], 1024
    return pl.pallas_call(
        _ada_body, grid=(n // tn,),
        in_specs=[pl.BlockSpec((bt, D_MODEL), lambda j: (0, 0)),
                  pl.BlockSpec((D_MODEL, tn), lambda j: (0, j)),
                  pl.BlockSpec((1, tn), lambda j: (0, j))],
        out_specs=pl.BlockSpec((bt, tn), lambda j: (0, j)),
        out_shape=jax.ShapeDtypeStruct((bt, n), F32),
        compiler_params=_params(("arbitrary",)), name="ada")(c, w_ada, b_ada.reshape(1, n))


def _norm_mod(x, g, sc, sh):
    ms = jnp.mean(x * x, axis=-1, keepdims=True)
    return (x * lax.rsqrt(ms + EPS) * g) * (1.0 + sc) + sh


def _inproj_body(x_ref, g_ref, sc_ref, sh_ref, w_ref, o_ref, h_scr):
    @pl.when(pl.program_id(1) == 0)
    def _():
        h_scr[...] = _norm_mod(x_ref[...], g_ref[...], sc_ref[...], sh_ref[...]).astype(BF16)

    o_ref[...] = jnp.dot(h_scr[...], w_ref[...], preferred_element_type=F32).astype(o_ref.dtype)


def _mod_spec(per_token, tm, tiles_per_batch, tile=lambda i: i):
    if per_token:
        return pl.BlockSpec((tm, D_MODEL), lambda i, *_: (tile(i), 0))
    return pl.BlockSpec((None, 1, D_MODEL), lambda i, *_: (tile(i) // tiles_per_batch, 0, 0))


def _ref_col_tile(j, tn):
    n_head, n_gate, n_kv = REF_COL_K // tn, (COL_K - COL_GA) // tn, (2 * KV_W) // tn
    assert n_head * tn == REF_COL_K and n_gate * tn == COL_K - COL_GA and n_kv * tn == 2 * KV_W
    return jnp.where(j < n_head, j, jnp.where(j < n_head + n_gate, j + n_kv, j - n_gate))


def _inproj(x, norm_g, sc, sh, w_bf, per_token, tm, seq):
    t, tn = x.shape[0], 1024
    tpb = max(seq // tm, 1)
    return pl.pallas_call(
        _inproj_body, grid=(t // tm, N_IN_COLS // tn),
        in_specs=[pl.BlockSpec((tm, D_MODEL), lambda i, j: (i, 0)),
                  pl.BlockSpec((1, D_MODEL), lambda i, j: (0, 0)),
                  _mod_spec(per_token, tm, tpb), _mod_spec(per_token, tm, tpb),
                  pl.BlockSpec((D_MODEL, tn), lambda i, j: (0, _ref_col_tile(j, tn)))],
        out_specs=pl.BlockSpec((tm, tn), lambda i, j: (i, j)),
        out_shape=jax.ShapeDtypeStruct((t, N_IN_COLS), BF16),
        scratch_shapes=[pltpu.VMEM((tm, D_MODEL), BF16)],
        compiler_params=_params(("arbitrary", "arbitrary")), name="inproj")(x, norm_g, sc, sh, w_bf)


def _norm_rope_all(xs, gains, c, s1, s2):
    ms = [jnp.mean(x * x, axis=-1, keepdims=True) for x in xs]
    ys = [x * lax.rsqrt(m + EPS) * g for x, m, g in zip(xs, ms, gains)]
    up = [pltpu.roll(y, ROT_DIMS // 2, 1) for y in ys]
    dn = [pltpu.roll(y, HEAD_DIM - ROT_DIMS // 2, 1) for y in ys]
    return [(y * c + u * s1 + d * s2).astype(BF16) for y, u, d in zip(ys, up, dn)]


def _rope_tables(pos):
    half = ROT_DIMS // 2
    inv = ROPE_THETA ** (-2.0 * jnp.arange(half, dtype=F32) / ROT_DIMS)
    ang = pos.astype(F32)[:, None] * inv[None, :]
    cos, sin = jnp.cos(ang), jnp.sin(ang)
    n = pos.shape[0]
    rest = HEAD_DIM - ROT_DIMS
    c = jnp.concatenate([cos, cos, jnp.ones((n, rest), F32)], axis=1)
    s1 = jnp.concatenate([jnp.zeros((n, half), F32), sin, jnp.zeros((n, rest), F32)], axis=1)
    s2 = jnp.concatenate([-sin, jnp.zeros((n, half + rest), F32)], axis=1)
    return c, s1, s2


def _lru_body(xa_ref, ya_ref, ga_ref, cs_ref, h0_ref, cw_ref, cb_ref, wra_ref, bra_ref, wri_ref, bri_ref,
              lam_ref, o_ref, nh_ref, nc_ref, xpad, a_scr, b_scr, g_scr, *, seq, rc, nsub):
    pad = SUBLANES
    xpad[0:pad, :] = jnp.zeros((pad, nsub * LRU_BS), F32)
    xpad[pad - (CONV_W - 1):pad, :] = cs_ref[...]
    xpad[pad:pad + seq, :] = xa_ref[...].astype(F32)
    nc_ref[...] = xpad[seq + pad - (CONV_W - 1):seq + pad, :]

    z = -lam_ref[...]
    softplus = jnp.maximum(z, 0.0) + jnp.log1p(jnp.exp(-jnp.abs(z)))
    clam_all = -LRU_C * softplus
    for sb in range(nsub):
        ls = slice(sb * LRU_BS, (sb + 1) * LRU_BS)
        clam = clam_all[:, ls]
        wra = wra_ref[sb].astype(BF16)
        wri = wri_ref[sb].astype(BF16)
        for c0 in range(0, seq, rc):
            slab = xpad[c0:c0 + pad + rc, ls]
            xc = cb_ref[:, ls]
            for tap in range(CONV_W):
                back = CONV_W - 1 - tap
                shifted = slab if back == 0 else pltpu.roll(slab, back, 0)
                xc = xc + shifted[pad:pad + rc, :] * cw_ref[tap:tap + 1, ls]
            xcb = xc.astype(BF16)
            r = jax.nn.sigmoid(jnp.dot(xcb, wra, preferred_element_type=F32) + bra_ref[sb])
            ig = jax.nn.sigmoid(jnp.dot(xcb, wri, preferred_element_type=F32) + bri_ref[sb])
            log_a = r * clam
            a_scr[c0:c0 + rc, ls] = jnp.exp(log_a)
            th = jnp.tanh(log_a)
            mult = jnp.sqrt(jnp.maximum(-2.0 * th / (1.0 - th), 0.0))
            b_scr[c0:c0 + rc, ls] = mult * (ig * xc)
            ya = ya_ref[c0:c0 + rc, ls].astype(F32)
            ga = ga_ref[c0:c0 + rc, ls].astype(F32)
            g_scr[c0:c0 + rc, ls] = jax.nn.sigmoid(ga) * jax.nn.gelu(ya)

    rows = 2 * SUBLANES
    rid = lax.broadcasted_iota(I32, (SUBLANES, LRU_BS), 0)

    def step(it, carry):
        r0 = pl.multiple_of(it * rows, rows)
        out = []
        for sb in range(nsub):
            ls = slice(sb * LRU_BS, (sb + 1) * LRU_BS)
            c = carry[:, ls]
            hs = []
            for half in range(rows // SUBLANES):
                rh = pl.multiple_of(r0 + half * SUBLANES, SUBLANES)
                a = a_scr[pl.ds(rh, SUBLANES), ls]
                b = b_scr[pl.ds(rh, SUBLANES), ls]
                for d in (1, 2, 4):
                    keep = rid >= d
                    a_s = jnp.where(keep, pltpu.roll(a, d, 0), 1.0)
                    b_s = jnp.where(keep, pltpu.roll(b, d, 0), 0.0)
                    b = a * b_s + b
                    a = a * a_s
                hs.append(b + a * c)
                c = a[SUBLANES - 1:SUBLANES, :] * c + b[SUBLANES - 1:SUBLANES, :]
            out.append(c)
            h = jnp.concatenate(hs, axis=0)
            o_ref[pl.ds(r0, rows), ls] = (g_scr[pl.ds(r0, rows), ls] * h).astype(o_ref.dtype)
        return jnp.concatenate(out, axis=1) if nsub > 1 else out[0]

    nh_ref[...] = lax.fori_loop(0, seq // rows, step, h0_ref[...], unroll=min(2, seq // rows))


def _lru(proj, conv_state, h0, conv_w, conv_b, w_ra, b_ra, w_ri, b_ri, lam, batch, seq, nsub):
    t = proj.shape[0]
    rc = min(seq, 256)
    cw = nsub * LRU_BS
    col = lambda off: (lambda b, c: (b, off // cw + c))
    vec = lambda: pl.BlockSpec((1, cw), lambda b, c: (0, c))
    blk3 = lambda r: pl.BlockSpec((None, r, cw), lambda b, c: (b, 0, c))
    wblk = lambda: pl.BlockSpec((nsub, LRU_BS, LRU_BS), lambda b, c: (c, 0, 0))
    bblk = lambda: pl.BlockSpec((nsub, 1, LRU_BS), lambda b, c: (c, 0, 0))
    return pl.pallas_call(
        functools.partial(_lru_body, seq=seq, rc=rc, nsub=nsub), grid=(batch, LRU_BLOCKS // nsub),
        in_specs=[pl.BlockSpec((seq, cw), col(COL_XA)), pl.BlockSpec((seq, cw), col(COL_YA)),
                  pl.BlockSpec((seq, cw), col(COL_GA)), blk3(CONV_W - 1), blk3(1),
                  pl.BlockSpec((CONV_W, cw), lambda b, c: (0, c)), vec(), wblk(), bblk(), wblk(), bblk(), vec()],
        out_specs=[pl.BlockSpec((seq, cw), lambda b, c: (b, c)), blk3(1), blk3(CONV_W - 1)],
        out_shape=[jax.ShapeDtypeStruct((t, D_MODEL), BF16), jax.ShapeDtypeStruct((batch, 1, D_MODEL), F32),
                   jax.ShapeDtypeStruct((batch, CONV_W - 1, D_MODEL), F32)],
        scratch_shapes=[pltpu.VMEM((seq + SUBLANES, cw), F32), pltpu.VMEM((seq, cw), F32),
                        pltpu.VMEM((seq, cw), F32), pltpu.VMEM((seq, cw), F32)],
        compiler_params=_params(("arbitrary", "arbitrary")), name="lru")(
            proj, proj, proj, conv_state, h0.reshape(batch, 1, D_MODEL), conv_w, conv_b.reshape(1, D_MODEL),
            w_ra, b_ra.reshape(LRU_BLOCKS, 1, LRU_BS), w_ri, b_ri.reshape(LRU_BLOCKS, 1, LRU_BS),
            lam.reshape(1, D_MODEL))


def _attn_body(*refs, banded, qt, npv, subs):
    if banded:
        (sink_ref, q_ref, kc_ref, vp_ref, vc_ref, gb_ref, ma_ref, qg_ref, kg_ref, c_ref, s1_ref, s2_ref,
         o_ref, kn_ref, kprev) = refs
    else:
        (sink_ref, q_ref, kp_ref, kc_ref, vp_ref, vc_ref, gb_ref, ma_ref, qg_ref, kg_ref, c_ref, s1_ref, s2_ref,
         o_ref, kn_ref) = refs
    nt = (((1,), (1,)), ((), ()))
    i = pl.program_id(1)
    log2e = math.log2(math.e)
    if banded:
        nk = npv + qt
        qc = lax.broadcasted_iota(I32, (qt, nk), 0) // CHUNK
        col = lax.broadcasted_iota(I32, (qt, nk), 1)
        band = ((col < npv) & (col // CHUNK >= qc)) | ((col >= npv) & ((col - npv) // CHUNK <= qc))

        @pl.when(i == 0)
        def _():
            kprev[...] = jnp.zeros_like(kprev)

    def logits2(q, k, ok):
        s = lax.dot_general(q, k, nt, preferred_element_type=F32) * (ATT_SCALE * log2e)
        return s if ok is None else jnp.where(ok, s, NEG)

    prev_k = None
    for sub in range(subs):
        rs = slice(sub * qt, (sub + 1) * qt)
        tab = (c_ref[rs, :], s1_ref[rs, :], s2_ref[rs, :])
        slabs = ([q_ref[rs, h * HEAD_DIM:(h + 1) * HEAD_DIM].astype(F32) for h in range(N_HEADS)]
                 + [kc_ref[rs, g * HEAD_DIM:(g + 1) * HEAD_DIM].astype(F32) for g in range(N_KV)])
        normed = _norm_rope_all(slabs, [qg_ref[...]] * N_HEADS + [kg_ref[...]] * N_KV, *tab)
        if banded:
            valid = band if sub > 0 else band & ((col >= npv) | (i > 0))
        units = []
        for g in range(N_KV):
            gs = slice(g * HEAD_DIM, (g + 1) * HEAD_DIM)
            kc = normed[N_HEADS + g]
            vc = vc_ref[rs, gs].astype(BF16)
            if sub > 0:
                kp, vp = prev_k[g], vc_ref[(sub - 1) * qt:sub * qt, gs].astype(BF16)
            elif banded:
                kp, vp = kprev[:, gs], vp_ref[:, gs].astype(BF16)
            else:
                kp, vp = kp_ref[:, gs].astype(BF16), vp_ref[:, gs].astype(BF16)
            if sub == subs - 1:
                kn_ref[:, gs] = kc
                if banded:
                    kprev[:, gs] = kc
            if banded:
                parts = [(jnp.concatenate([kp, kc], axis=0), jnp.concatenate([vp, vc], axis=0), valid)]
            else:
                parts = [(kp, vp, None), (kc, vc, None)]
            parts = [(k, jnp.concatenate([v, jnp.ones(v.shape, BF16)], axis=1), ok) for k, v, ok in parts]
            for h in range(g * GROUP, (g + 1) * GROUP):
                units.append(dict(h=h, q=normed[h], sk=jnp.full((qt, 1), sink_ref[h] * log2e, F32), parts=parts))
        prev_k = normed[N_HEADS:]

        for wave in range(0, N_HEADS, ATTN_WAVE):
            active = units[wave:wave + ATTN_WAVE]
            for u in active:
                u["s"] = [logits2(u["q"], k, ok) for k, _, ok in u["parts"]]
            for u in active:
                m = u["sk"]
                for s in u["s"]:
                    m = jnp.maximum(m, jnp.max(s, -1, keepdims=True))
                u["m"] = m
            for u in active:
                u["p"] = [jnp.exp2(s - u["m"]).astype(BF16) for s in u["s"]]
            for u in active:
                acc = jnp.dot(u["p"][0], u["parts"][0][1], preferred_element_type=F32)
                for p, (_, v, _) in zip(u["p"][1:], u["parts"][1:]):
                    acc = acc + jnp.dot(p, v, preferred_element_type=F32)
                den = acc[:, HEAD_DIM:] + jnp.exp2(u["sk"] - u["m"])
                u["o"] = acc[:, :HEAD_DIM] / den
            for u in active:
                hs = slice(u["h"] * HEAD_DIM, (u["h"] + 1) * HEAD_DIM)
                gate = jax.nn.sigmoid(gb_ref[rs, hs].astype(F32))
                o_ref[rs, hs] = (ma_ref[rs, hs].astype(F32) + gate * u["o"]).astype(o_ref.dtype)


def _attn(sink, proj, ma, tabs, q_norm_g, k_norm_g, k_cache2, v_cache2, batch, seq, qt, subs):
    t = proj.shape[0]
    nq = seq // qt
    ns = nq // subs
    banded = k_cache2 is None
    row = lambda width, col: pl.BlockSpec((subs * qt, width), lambda b, i: (b * ns + i, col))
    vec = lambda: pl.BlockSpec((1, HEAD_DIM), lambda b, i: (0, 0))
    tab = lambda: pl.BlockSpec((subs * qt, HEAD_DIM), lambda b, i: (i, 0))
    if banded:
        npv = qt
        prev_specs = [pl.BlockSpec((npv, KV_W), lambda b, i: (b * nq + jnp.maximum(i * subs - 1, 0), COL_V // KV_W))]
        prev_args = [proj]
        scratch = [pltpu.VMEM((qt, KV_W), BF16)]
    else:
        npv = k_cache2.shape[0] // batch
        prev_specs = [pl.BlockSpec((npv, KV_W), lambda b, i: (b, 0))] * 2
        prev_args = [k_cache2, v_cache2]
        scratch = []
    in_specs = ([pl.BlockSpec(memory_space=pltpu.SMEM), row(Q_W, COL_Q // Q_W)]
                + prev_specs[:-1] + [row(KV_W, COL_K // KV_W), prev_specs[-1], row(KV_W, COL_V // KV_W),
                                     row(Q_W, COL_GB // Q_W), row(Q_W, 0), vec(), vec(), tab(), tab(), tab()])
    args = ([sink, proj] + prev_args[:-1] + [proj, prev_args[-1], proj, proj, ma,
                                             q_norm_g.reshape(1, HEAD_DIM), k_norm_g.reshape(1, HEAD_DIM), *tabs])
    return pl.pallas_call(
        functools.partial(_attn_body, banded=banded, qt=qt, npv=npv, subs=subs), grid=(batch, ns),
        in_specs=in_specs,
        out_specs=[pl.BlockSpec((subs * qt, Q_W), lambda b, i: (b * ns + i, 0)),
                   pl.BlockSpec((None, qt, KV_W), lambda b, i: (b, 0, 0))],
        out_shape=[jax.ShapeDtypeStruct((t, Q_W), BF16), jax.ShapeDtypeStruct((batch, qt, KV_W), BF16)],
        scratch_shapes=scratch,
        compiler_params=_params(("arbitrary", "arbitrary")), name="attn")(*args)


def _lane_min(x):
    return jnp.min(x, axis=-1, keepdims=True)


OUTPROJ_CHUNKS = 4


def _outproj_body(x_ref, m_ref, w_ref, g1_ref, n2_ref, sc_ref, sh_ref, wrc_ref, wrh_ref, br_ref, cin_ref,
                  x1_ref, h2_ref, mf_ref, mi_ref, cnt_ref, base_scr, y_even, y_odd, *, tm):
    i = pl.program_id(0)

    @pl.when(i == 0)
    def _():
        base_scr[...] = cin_ref[...]
        y_odd[...] = jnp.zeros_like(y_odd)

    cw = D_MODEL // OUTPROJ_CHUNKS
    for parity, (y_new, y_old) in enumerate(((y_even, y_odd), (y_odd, y_even))):
        @pl.when(i % 2 == parity)
        def _(y_new=y_new, y_old=y_old):
            def product(c):
                y_new[:, c * cw:(c + 1) * cw] = jnp.dot(m_ref[...], w_ref[:, c * cw:(c + 1) * cw],
                                                        preferred_element_type=F32)

            _outproj_finish(y_old, i > 0, product, x_ref, g1_ref, n2_ref, sc_ref, sh_ref, wrc_ref, wrh_ref, br_ref,
                            x1_ref, h2_ref, mf_ref, mi_ref, cnt_ref, base_scr, tm)


def _outproj_finish(y_ref, live, product, x_ref, g1_ref, n2_ref, sc_ref, sh_ref, wrc_ref, wrh_ref, br_ref,
                    x1_ref, h2_ref, mf_ref, mi_ref, cnt_ref, base_scr, tm):
    product(0)
    x1 = x_ref[...] + g1_ref[...] * y_ref[...]
    x1_ref[...] = x1
    h2 = _norm_mod(x1, n2_ref[...], sc_ref[...], sh_ref[...])
    h2_ref[...] = h2
    product(1)

    hi = h2.astype(BF16)
    lo = (h2 - hi.astype(F32)).astype(BF16)
    both = jnp.dot(hi, wrc_ref[...], preferred_element_type=F32)
    logits = (both[:, :LANES] + jnp.dot(lo, wrh_ref[...], preferred_element_type=F32) + both[:, LANES:]) + br_ref[...]
    lane = lax.broadcasted_iota(I32, (tm, LANES), 1).astype(F32)
    ninf = -jnp.inf
    big = float(LANES)
    is_g = lane < N_EXP_GROUPS
    lg = jnp.where(is_g, logits, ninf)
    mg = jnp.max(lg, -1, keepdims=True)
    gsel = _lane_min(jnp.where(lg == mg, lane, big))
    p_g = 1.0 / jnp.sum(jnp.where(is_g, jnp.exp(lg - mg), 0.0), -1, keepdims=True)
    lo_lane = N_EXP_GROUPS + EXP_PER_GROUP * gsel
    le = jnp.where((lane >= lo_lane) & (lane < lo_lane + EXP_PER_GROUP), logits, ninf)
    m1 = jnp.max(le, -1, keepdims=True)
    i1 = _lane_min(jnp.where(le == m1, lane, big))
    le2 = jnp.where(lane == i1, ninf, le)
    m2 = jnp.max(le2, -1, keepdims=True)
    i2 = _lane_min(jnp.where(le2 == m2, lane, big))
    e21 = jnp.exp(m2 - m1)
    w1 = p_g / (1.0 + e21)
    w2 = p_g * e21 / (1.0 + e21)
    e1, e2 = i1 - N_EXP_GROUPS, i2 - N_EXP_GROUPS
    product(2)

    onehot = ((lane == e1) | (lane == e2))
    row = lax.broadcasted_iota(I32, (tm, tm), 0)
    colm = lax.broadcasted_iota(I32, (tm, tm), 1)
    lower = jnp.where(row > colm, 1.0, 0.0).astype(BF16)
    before = jnp.dot(lower, jnp.where(onehot, 1.0, 0.0).astype(BF16), preferred_element_type=F32) + base_scr[...]
    r1 = jnp.sum(jnp.where(lane == e1, before, 0.0), -1, keepdims=True)
    r2 = jnp.sum(jnp.where(lane == e2, before, 0.0), -1, keepdims=True)
    base_scr[...] = base_scr[...] + jnp.sum(jnp.where(onehot & live, 1.0, 0.0), axis=0, keepdims=True)
    cnt_ref[...] = base_scr[...]

    mf_ref[...] = jnp.where(lane == 0.0, w1, jnp.where(lane == 1.0, w2, 0.0))
    mi = jnp.where(lane == 0.0, e1, jnp.where(lane == 1.0, e2, jnp.where(lane == 2.0, r1, jnp.where(lane == 3.0, r2, 0.0))))
    mi_ref[...] = mi.astype(I32)
    product(3)


def _outproj(x, merged, w_bf, g1, norm2_g, sc, sh, wr_cat, wr_hi, br, cnt_in, per_token, tm, seq):
    t = x.shape[0]
    tpb = max(seq // tm, 1)
    n = t // tm
    done = lambda i: jnp.maximum(i - 1, 0)
    row = lambda: pl.BlockSpec((tm, D_MODEL), lambda i: (done(i), 0))
    full = lambda shape: pl.BlockSpec(shape, lambda i: (0, 0))
    meta = lambda: pl.BlockSpec((tm, LANES), lambda i: (done(i), 0))
    mod = lambda: _mod_spec(per_token, tm, tpb, done)
    return pl.pallas_call(
        functools.partial(_outproj_body, tm=tm), grid=(n + 1,),
        in_specs=[row(), pl.BlockSpec((tm, D_MODEL), lambda i: (jnp.minimum(i, n - 1), 0)),
                  pl.BlockSpec((D_MODEL, D_MODEL), lambda i: (0, 0), pipeline_mode=pl.Buffered(1)),
                  mod(), full((1, D_MODEL)), mod(), mod(),
                  full((D_MODEL, 2 * LANES)), full((D_MODEL, LANES)), full((1, LANES)), full((1, LANES))],
        out_specs=[row(), row(), meta(), meta(), full((1, LANES))],
        out_shape=[jax.ShapeDtypeStruct((t, D_MODEL), F32), jax.ShapeDtypeStruct((t, D_MODEL), F32),
                   jax.ShapeDtypeStruct((t, LANES), F32), jax.ShapeDtypeStruct((t, LANES), I32),
                   jax.ShapeDtypeStruct((1, LANES), F32)],
        scratch_shapes=[pltpu.VMEM((1, LANES), F32), pltpu.VMEM((tm, D_MODEL), F32), pltpu.VMEM((tm, D_MODEL), F32)],
        compiler_params=_params(("arbitrary",)), name="outproj")(
            x, merged, w_bf, g1, norm2_g, sc, sh, wr_cat, wr_hi, br, cnt_in)


def _lane_cumsum(x):
    lane = lax.broadcasted_iota(I32, x.shape, 1)
    for d in (1, 2, 4, 8, 16):
        x = x + jnp.where(lane >= d, pltpu.roll(x, d, 1), 0.0)
    return x


def _plan_body(cnt_ref, mi_ref, dest_ref, bexp_ref, pend_ref, *, tp, blk, nb_pad):
    cnt = jnp.broadcast_to(cnt_ref[...], (SUBLANES, LANES))
    lane8 = lax.broadcasted_iota(I32, (SUBLANES, LANES), 1)
    padded = jnp.where(lane8 < N_EXPERTS, jnp.ceil(cnt / blk) * blk, 0.0)
    pend = _lane_cumsum(padded)
    pstart = (pend - padded)[0:1, :]
    mi = mi_ref[...]
    lane = lax.broadcasted_iota(I32, (tp, LANES), 1)
    e1, e2, r1, r2 = mi[:, 0:1], mi[:, 1:2], mi[:, 2:3], mi[:, 3:4]
    d1 = jnp.sum(jnp.where(lane == e1, pstart, 0.0), -1, keepdims=True).astype(I32) + r1
    d2 = jnp.sum(jnp.where(lane == e2, pstart, 0.0), -1, keepdims=True).astype(I32) + r2
    dest_ref[...] = jnp.where(lane == 0, d1, jnp.where(lane == 1, d2, 0))

    @pl.when(pl.program_id(0) == 0)
    def _():
        jb = lax.broadcasted_iota(I32, (nb_pad, LANES), 0).astype(F32) * blk
        ln = lax.broadcasted_iota(I32, (nb_pad, LANES), 1)
        ends = jnp.where((ln < N_EXPERTS) & (pend[0:1, :] <= jb), 1.0, 0.0)
        be = jnp.minimum(jnp.sum(ends, -1, keepdims=True), N_EXPERTS - 1.0)
        used = pend[0:1, N_EXPERTS - 1:N_EXPERTS] / blk
        bexp_ref[...] = jnp.where(ln == 0, be, jnp.where(ln == 1, used, 0.0)).astype(I32)
        pend_ref[...] = pend.astype(I32)


def _plan(cnt, mi, blk, nb_pad):
    t = mi.shape[0]
    tp = min(t, 4096)
    return pl.pallas_call(
        functools.partial(_plan_body, tp=tp, blk=float(blk), nb_pad=nb_pad), grid=(t // tp,),
        in_specs=[pl.BlockSpec((1, LANES), lambda i: (0, 0)), pl.BlockSpec((tp, LANES), lambda i: (i, 0))],
        out_specs=[pl.BlockSpec((tp, LANES), lambda i: (i, 0)), pl.BlockSpec((nb_pad, LANES), lambda i: (0, 0)),
                   pl.BlockSpec((SUBLANES, LANES), lambda i: (0, 0))],
        out_shape=[jax.ShapeDtypeStruct((t, LANES), I32), jax.ShapeDtypeStruct((nb_pad, LANES), I32),
                   jax.ShapeDtypeStruct((SUBLANES, LANES), I32)],
        compiler_params=_params(("arbitrary",)), name="plan")(cnt, mi)


DISPATCH_SLOTS = 4


def _dispatch_body(*refs, tm, blk, bounds):
    pend_ref, dest_ref = refs[:2]
    h2_refs = refs[2:2 + len(bounds)]
    xs_ref, zbuf, tiles, lsem, rsem, zsem = refs[2 + len(bounds):]
    nsteps = bounds[-1][1]
    i = pl.program_id(0)

    def zero_copy(e):
        return pltpu.make_async_copy(zbuf, xs_ref.at[pl.ds(pl.multiple_of(pend_ref[e] - blk, blk), blk)], zsem)

    def has_rows(e):
        return pend_ref[e] > jnp.where(e > 0, pend_ref[jnp.maximum(e - 1, 0)], 0)

    @pl.when(i == 0)
    def _():
        zbuf[...] = jnp.zeros_like(zbuf)

        def zstart(e, c):
            @pl.when(has_rows(e))
            def _():
                zero_copy(e).start()
            return c

        def zwait(e, c):
            @pl.when(has_rows(e))
            def _():
                zero_copy(e).wait()
            return c

        lax.fori_loop(0, N_EXPERTS, zstart, 0)
        lax.fori_loop(0, N_EXPERTS, zwait, 0)

        def tail_copy(j):
            return pltpu.make_async_copy(zbuf, xs_ref.at[pl.ds(pl.multiple_of(j * blk, blk), blk)], zsem)

        def tstart(j, c):
            tail_copy(j).start()
            return c

        def twait(j, c):
            tail_copy(j).wait()
            return c

        used = pend_ref[N_EXPERTS - 1] // blk
        lax.fori_loop(used, xs_ref.shape[0] // blk, tstart, 0)
        lax.fori_loop(used, xs_ref.shape[0] // blk, twait, 0)

    def load_start(j):
        s = j % DISPATCH_SLOTS
        for (lo, hi), h2_ref in zip(bounds, h2_refs):
            @pl.when((j >= lo) & (j < hi))
            def _(lo=lo, h2_ref=h2_ref):
                pltpu.make_async_copy(h2_ref.at[pl.ds(pl.multiple_of((j - lo) * tm, tm), tm)], tiles.at[s],
                                      lsem.at[s]).start()

    def load_wait(j):
        s = j % DISPATCH_SLOTS
        pltpu.make_async_copy(h2_refs[0].at[pl.ds(0, tm)], tiles.at[s], lsem.at[s]).wait()

    def wait_rows(j):
        s = j % DISPATCH_SLOTS
        for _ in range(2):
            pltpu.make_async_copy(tiles.at[s], xs_ref.at[pl.ds(0, tm)], rsem.at[s]).wait()

    @pl.when(i == 0)
    def _():
        load_start(0)
        if nsteps > 1:
            load_start(1)

    @pl.when(i >= 2)
    def _():
        wait_rows(i - 2)

    @pl.when(i + 2 < nsteps)
    def _():
        load_start(i + 2)

    load_wait(i)
    slot = i % DISPATCH_SLOTS

    def start(r, c):
        for k in range(2):
            pltpu.make_async_copy(tiles.at[slot, pl.ds(r, 1)], xs_ref.at[pl.ds(dest_ref[0, k * tm + r], 1)],
                                  rsem.at[slot]).start(priority=k)
        return c

    lax.fori_loop(0, tm, start, 0, unroll=8)

    @pl.when(i == nsteps - 1)
    def _():
        if nsteps > 1:
            wait_rows(i - 1)
        wait_rows(i)


def _dispatch(pend, dest3, h2_list, n_rows, tm, blk):
    bounds, lo = [], 0
    for h2 in h2_list:
        bounds.append((lo, lo + h2.shape[0] // tm))
        lo = bounds[-1][1]
    return pl.pallas_call(
        functools.partial(_dispatch_body, tm=tm, blk=blk, bounds=tuple(bounds)), grid=(lo,),
        in_specs=[pl.BlockSpec(memory_space=pltpu.SMEM),
                  pl.BlockSpec((None, 1, 2 * tm), lambda i: (i, 0, 0), memory_space=pltpu.SMEM)]
                 + [pl.BlockSpec(memory_space=pl.ANY)] * len(h2_list),
        out_specs=pl.BlockSpec(memory_space=pl.ANY),
        out_shape=jax.ShapeDtypeStruct((n_rows, D_MODEL), F32),
        scratch_shapes=[pltpu.VMEM((blk, D_MODEL), F32), pltpu.VMEM((DISPATCH_SLOTS, tm, D_MODEL), F32),
                        pltpu.SemaphoreType.DMA((DISPATCH_SLOTS,)), pltpu.SemaphoreType.DMA((DISPATCH_SLOTS,)),
                        pltpu.SemaphoreType.DMA(())],
        compiler_params=_params(("arbitrary",)), name="dispatch")(pend, dest3, *h2_list)


def _moe_body(bexp_ref, used_ref, xs_ref, wu_hbm, wd_hbm, ys_ref, wu_f, wd_f, wu_bf, wd_bf, slot_ref, sem):
    j = pl.program_id(0)
    used = used_ref[0]
    e = bexp_ref[j]
    live = j < used
    changed = (j == 0) | (e != bexp_ref[jnp.maximum(j - 1, 0)])

    def fetch(expert, slot):
        return (pltpu.make_async_copy(wu_hbm.at[expert], wu_f.at[slot], sem.at[slot]),
                pltpu.make_async_copy(wd_hbm.at[expert], wd_f.at[slot], sem.at[slot]))

    @pl.when((j == 0) & live)
    def _():
        slot_ref[0] = 0
        for copy in fetch(e, 0):
            copy.start()

    @pl.when(changed & live)
    def _():
        slot = slot_ref[0]
        for copy in fetch(e, slot):
            copy.wait()
        last = pl.num_programs(0) - 1
        nxt = lax.while_loop(lambda c: (c < used) & (bexp_ref[jnp.minimum(c, last)] == e), lambda c: c + 1, j + 1)

        @pl.when(nxt < used)
        def _():
            for copy in fetch(bexp_ref[jnp.minimum(nxt, last)], 1 - slot):
                copy.start()

        wu_bf[...] = wu_f[slot].astype(BF16)
        wd_bf[...] = wd_f[slot].astype(BF16)
        slot_ref[0] = 1 - slot

    @pl.when(live)
    def _():
        up = jnp.dot(xs_ref[...].astype(BF16), wu_bf[...], preferred_element_type=F32)
        act = jax.nn.silu(up[:, :D_EXPERT]) * up[:, D_EXPERT:]
        ys_ref[...] = jnp.dot(act.astype(BF16), wd_bf[...], preferred_element_type=F32)

    @pl.when(j >= used)
    def _():
        ys_ref[...] = jnp.zeros_like(ys_ref)


def _moe(bexp, used, xs, w_up, w_down, blk):
    p = xs.shape[0]
    grid_spec = pltpu.PrefetchScalarGridSpec(
        num_scalar_prefetch=2, grid=(p // blk,),
        in_specs=[pl.BlockSpec((blk, D_MODEL), lambda j, be, us: (j, 0)),
                  pl.BlockSpec(memory_space=pl.ANY), pl.BlockSpec(memory_space=pl.ANY)],
        out_specs=pl.BlockSpec((blk, D_MODEL), lambda j, be, us: (j, 0)),
        scratch_shapes=[pltpu.VMEM((2, D_MODEL, 2 * D_EXPERT), F32), pltpu.VMEM((2, D_EXPERT, D_MODEL), F32),
                        pltpu.VMEM((D_MODEL, 2 * D_EXPERT), BF16), pltpu.VMEM((D_EXPERT, D_MODEL), BF16),
                        pltpu.SMEM((1,), I32), pltpu.SemaphoreType.DMA((2,))])
    return pl.pallas_call(
        _moe_body, grid_spec=grid_spec, out_shape=jax.ShapeDtypeStruct((p, D_MODEL), F32),
        compiler_params=_params(("arbitrary",)), name="moe")(bexp, used, xs, w_up, w_down)


def _combine_body(dest_ref, dnext_ref, x1_ref, mf_ref, g2_ref, ys_ref, o_ref, ybuf, sem, *, tm, nsteps):
    i = pl.program_id(0)

    def issue(dref, s):
        def start(r, c):
            for k in range(2):
                pltpu.make_async_copy(ys_ref.at[pl.ds(dref[0, k * tm + r], 1)], ybuf.at[s, k, pl.ds(r, 1)],
                                      sem.at[s]).start(priority=k)
            return c

        lax.fori_loop(0, tm, start, 0, unroll=8)

    @pl.when(i == 0)
    def _():
        issue(dest_ref, 0)

    for s in range(2):
        @pl.when(i % 2 == s)
        def _(s=s):
            @pl.when(i + 1 < nsteps)
            def _():
                issue(dnext_ref, 1 - s)

            for k in range(2):
                pltpu.make_async_copy(ys_ref.at[pl.ds(0, tm)], ybuf.at[s, k], sem.at[s]).wait()
            mf = mf_ref[...]
            moe = mf[:, 0:1] * ybuf[s, 0] + mf[:, 1:2] * ybuf[s, 1]
            o_ref[...] = x1_ref[...] + g2_ref[...] * moe


def _combine(dest3, x1, mf, g2, ys, per_token, tm, seq):
    t = x1.shape[0]
    tpb = max(seq // tm, 1)
    nsteps = t // tm
    return pl.pallas_call(
        functools.partial(_combine_body, tm=tm, nsteps=nsteps), grid=(nsteps,),
        in_specs=[pl.BlockSpec((None, 1, 2 * tm), lambda i: (i, 0, 0), memory_space=pltpu.SMEM),
                  pl.BlockSpec((None, 1, 2 * tm), lambda i: (jnp.minimum(i + 1, nsteps - 1), 0, 0),
                               memory_space=pltpu.SMEM),
                  pl.BlockSpec((tm, D_MODEL), lambda i: (i, 0)),
                  pl.BlockSpec((tm, LANES), lambda i: (i, 0)),
                  _mod_spec(per_token, tm, tpb),
                  pl.BlockSpec(memory_space=pl.ANY)],
        out_specs=pl.BlockSpec((tm, D_MODEL), lambda i: (i, 0)),
        out_shape=jax.ShapeDtypeStruct((t, D_MODEL), F32),
        scratch_shapes=[pltpu.VMEM((2, 2, tm, D_MODEL), F32), pltpu.SemaphoreType.DMA((2,))],
        compiler_params=_params(("arbitrary",)), name="combine")(dest3, dest3, x1, mf, g2, ys)


def _group_mod(mod6, per_token, seq):
    if per_token:
        return jnp.repeat(mod6, seq, axis=0)
    return mod6[:, None, :]


def _front(x, mods, pos, conv_state, h0, k_cache, v_cache, w, cnt_in, seq, tm_in, tm_out, nsub):
    batch = x.shape[0]
    t = batch * seq
    xt = x.reshape(t, D_MODEL)
    per_token = seq < tm_out
    sh1, sc1, g1, sh2, sc2, g2 = [_group_mod(m, per_token, seq) for m in mods]

    proj = _inproj(xt, w["norm1_g"], sc1, sh1, w["w_in"], per_token, tm_in, seq)

    ma, new_h, new_conv = _lru(proj, conv_state, h0, w["conv_w"], w["conv_b"], w["w_ra"], w["b_ra"],
                               w["w_ri"], w["b_ri"], w["lru_lambda"], batch, seq, nsub)

    tabs = _rope_tables(pos)
    keep = min(WINDOW, seq)
    if k_cache is None:
        merged, kn = _attn(w["attn_sink"], proj, ma, tabs, w["q_norm_g"], w["k_norm_g"], None, None, batch, seq, keep,
                           subs=2 if seq % (2 * keep) == 0 else 1)
    else:
        rows = k_cache.shape[1]
        merged, kn = _attn(w["attn_sink"], proj, ma, tabs, w["q_norm_g"], w["k_norm_g"],
                           k_cache.reshape(batch * rows, KV_W), v_cache.reshape(batch * rows, KV_W), batch, seq, keep,
                           subs=1)
    v3 = proj.reshape(batch, seq, N_IN_COLS)[:, seq - keep:, COL_V:COL_V + KV_W]
    new_k = kn.astype(F32).reshape(batch, keep, N_KV, HEAD_DIM)
    new_v = v3.astype(F32).reshape(batch, keep, N_KV, HEAD_DIM)
    if k_cache is not None:
        new_k = jnp.concatenate([k_cache, new_k], axis=1)[:, -rows:]
        new_v = jnp.concatenate([v_cache, new_v], axis=1)[:, -rows:]

    x1, h2, mf, mi, cnt = _outproj(xt, merged, w["w_out"], g1, w["norm2_g"], sc2, sh2,
                                   w["wr_cat"], w["wr_hi"], w["br"], cnt_in, per_token, tm_out, seq)
    return dict(batch=batch, seq=seq, t=t, per_token=per_token, tm_out=tm_out, x1=x1, h2=h2, mf=mf, mi=mi, cnt=cnt,
                g2=g2, state=(new_h.reshape(batch, D_MODEL), new_conv, new_k, new_v))


def _flat_dest(dest, tm):
    t = dest.shape[0]
    return dest[:, 0:2].reshape(t // tm, tm, 2).transpose(0, 2, 1).reshape(t // tm, 1, 2 * tm)


def _experts(groups, w, blk):
    total = sum(gr["t"] for gr in groups)
    n_blocks = -(-(2 * total) // blk) + N_EXPERTS
    nb_pad = -(-n_blocks // SUBLANES) * SUBLANES
    cnt = groups[-1]["cnt"]
    tmd = min(min(gr["t"] for gr in groups), 256)
    dest_d = []
    for gr in groups:
        dest, bmeta, pend = _plan(cnt, gr["mi"], blk, nb_pad)
        gr["dest_c"] = _flat_dest(dest, gr["tm_out"])
        dest_d.append(_flat_dest(dest, tmd))
    xs = _dispatch(pend[0, :N_EXPERTS], jnp.concatenate(dest_d, axis=0), [gr["h2"] for gr in groups],
                   n_blocks * blk, tmd, blk)
    ys = _moe(bmeta[:n_blocks, 0], bmeta[0:1, 1], xs, w["w_exp_up"], w["w_exp_down"], blk)
    return [_combine(gr["dest_c"], gr["x1"], gr["mf"], gr["g2"], ys, gr["per_token"], gr["tm_out"], gr["seq"])
            .reshape(gr["batch"], gr["seq"], D_MODEL) for gr in groups]


def kernel(x_prompt, x_sample, state_lru_h, state_lru_conv, cache_swa_k, cache_swa_v, c_prompt, c_sample,
           norm1_g, norm2_g, w_ada, b_ada, w_in, conv_w, conv_b, w_ra, b_ra, w_ri, b_ri, lru_lambda,
           q_norm_g, k_norm_g, attn_sink, w_out, w_router_g, b_router_g, w_router_e, b_router_e,
           w_exp_up, w_exp_down):
    assert norm1_g.shape[0] == 1, "single-layer trunk"
    bp, sp, _ = x_prompt.shape
    bs, ss, _ = x_sample.shape

    mod = _ada(jnp.concatenate([c_prompt, c_sample], axis=0), w_ada[0], b_ada[0])
    mods_p = jnp.split(mod[:bp], 6, axis=-1)
    mods_s = jnp.split(mod[bp:], 6, axis=-1)

    pad = LANES - N_EXP_GROUPS - N_EXPERTS
    wr = jnp.concatenate([w_router_g[0], w_router_e[0], jnp.zeros((D_MODEL, pad), F32)], axis=1)
    wr_hi = wr.astype(BF16)
    wr_lo = (wr - wr_hi.astype(F32)).astype(BF16)
    br = jnp.concatenate([b_router_g[0], b_router_e[0], jnp.zeros((pad,), F32)]).reshape(1, LANES)

    w = dict(norm1_g=norm1_g[0].reshape(1, D_MODEL), norm2_g=norm2_g[0].reshape(1, D_MODEL),
             w_in=w_in[0].astype(BF16), conv_w=conv_w[0], conv_b=conv_b[0], w_ra=w_ra[0], b_ra=b_ra[0],
             w_ri=w_ri[0], b_ri=b_ri[0], lru_lambda=lru_lambda[0], q_norm_g=q_norm_g[0], k_norm_g=k_norm_g[0],
             attn_sink=attn_sink[0], w_out=w_out[0].astype(BF16), wr_hi=wr_hi,
             wr_cat=jnp.concatenate([wr_hi, wr_lo], axis=1), br=br,
             w_exp_up=w_exp_up[0], w_exp_down=w_exp_down[0])

    prompt = _front(x_prompt, mods_p, jnp.arange(sp), jnp.zeros((bp, CONV_W - 1, D_MODEL), F32),
                    jnp.zeros((bp, D_MODEL), F32), None, None, w, jnp.zeros((1, LANES), F32), sp,
                    tm_in=min(1024, sp), tm_out=min(512, sp), nsub=4)
    sample = _front(x_sample, mods_s, PAST_LEN + jnp.arange(ss), state_lru_conv[0], state_lru_h[0],
                    cache_swa_k[0], cache_swa_v[0], w, prompt["cnt"], ss,
                    tm_in=bs * ss, tm_out=bs * ss, nsub=LRU_BLOCKS)
    yp, ys = _experts([prompt, sample], w, blk=256)
    ph, pc, pk, pv = prompt["state"]
    sh, sc, sk, sv = sample["state"]
    return (yp, ys, ph[None], pc[None], pk[None], pv[None], sh[None], sc[None], sk[None], sv[None])
```

```python
import functools
import math

import jax
import jax.numpy as jnp
from jax import lax
from jax.experimental import pallas as pl
from jax.experimental.pallas import tpu as pltpu

F32 = jnp.float32
BF16 = jnp.bfloat16
I32 = jnp.int32

D_MODEL = 2048
EPS = 1e-6
LANES = 128
SUBLANES = 8
CHUNK = 64
CONV_W = 4
LRU_BLOCKS = 16
LRU_BS = D_MODEL // LRU_BLOCKS
LRU_C = 8.0
HEAD_DIM = 128
N_HEADS = D_MODEL // HEAD_DIM
N_KV = 4
GROUP = N_HEADS // N_KV
WINDOW = 128
ROT_DIMS = HEAD_DIM // 4
ROPE_THETA = 500000.0
NEG = -1e30
N_EXP_GROUPS = 4
EXP_PER_GROUP = 8
N_EXPERTS = N_EXP_GROUPS * EXP_PER_GROUP
D_EXPERT = D_MODEL // 4
Q_W = N_HEADS * HEAD_DIM
KV_W = N_KV * HEAD_DIM
REF_COL_K = 2 * D_MODEL + Q_W
REF_COL_GA = REF_COL_K + 2 * KV_W
COL_XA, COL_YA, COL_Q = 0, D_MODEL, 2 * D_MODEL
COL_GA = COL_Q + Q_W
COL_GB = COL_GA + D_MODEL
COL_K = COL_GB + D_MODEL
COL_V = COL_K + KV_W
N_IN_COLS = COL_V + KV_W
PAST_LEN = 1024
ATT_SCALE = HEAD_DIM ** -0.5
ATTN_WAVE = 8
VMEM_LIMIT = 56 * 1024 * 1024


def _params(sem, vmem=VMEM_LIMIT):
    return pltpu.CompilerParams(dimension_semantics=sem, vmem_limit_bytes=vmem)


def _ada_body(c_ref, w_ref, b_ref, o_ref):
    c = c_ref[...]
    s = c * jax.nn.sigmoid(c)
    o_ref[...] = jnp.dot(s.astype(BF16), w_ref[...].astype(BF16), preferred_element_type=F32) + b_ref[...]


def _ada(c, w_ada, b_ada):
    bt, n, tn = c.shape[0], w_ada.shape[1], 1024
    return pl.pallas_call(
        _ada_body, grid=(n // tn,),
        in_specs=[pl.BlockSpec((bt, D_MODEL), lambda j: (0, 0)),
                  pl.BlockSpec((D_MODEL, tn), lambda j: (0, j)),
                  pl.BlockSpec((1, tn), lambda j: (0, j))],
        out_specs=pl.BlockSpec((bt, tn), lambda j: (0, j)),
        out_shape=jax.ShapeDtypeStruct((bt, n), F32),
        compiler_params=_params(("arbitrary",)), name="ada")(c, w_ada, b_ada.reshape(1, n))


def _norm_mod(x, g, sc, sh):
    ms = jnp.mean(x * x, axis=-1, keepdims=True)
    return (x * lax.rsqrt(ms + EPS) * g) * (1.0 + sc) + sh


def _inproj_body(x_ref, g_ref, sc_ref, sh_ref, w_ref, o_ref, h_scr):
    @pl.when(pl.program_id(1) == 0)
    def _():
        h_scr[...] = _norm_mod(x_ref[...], g_ref[...], sc_ref[...], sh_ref[...]).astype(BF16)

    o_ref[...] = jnp.dot(h_scr[...], w_ref[...], preferred_element_type=F32).astype(o_ref.dtype)


def _mod_spec(per_token, tm, tiles_per_batch, tile=lambda i: i):
    if per_token:
        return pl.BlockSpec((tm, D_MODEL), lambda i, *_: (tile(i), 0))
    return pl.BlockSpec((None, 1, D_MODEL), lambda i, *_: (tile(i) // tiles_per_batch, 0, 0))


def _ref_col_tile(j, tn):
    n_head, n_gate, n_kv = REF_COL_K // tn, (COL_K - COL_GA) // tn, (2 * KV_W) // tn
    assert n_head * tn == REF_COL_K and n_gate * tn == COL_K - COL_GA and n_kv * tn == 2 * KV_W
    return jnp.where(j < n_head, j, jnp.where(j < n_head + n_gate, j + n_kv, j - n_gate))


def _inproj(x, norm_g, sc, sh, w_bf, per_token, tm, seq):
    t, tn = x.shape[0], 1024
    tpb = max(seq // tm, 1)
    return pl.pallas_call(
        _inproj_body, grid=(t // tm, N_IN_COLS // tn),
        in_specs=[pl.BlockSpec((tm, D_MODEL), lambda i, j: (i, 0)),
                  pl.BlockSpec((1, D_MODEL), lambda i, j: (0, 0)),
                  _mod_spec(per_token, tm, tpb), _mod_spec(per_token, tm, tpb),
                  pl.BlockSpec((D_MODEL, tn), lambda i, j: (0, _ref_col_tile(j, tn)))],
        out_specs=pl.BlockSpec((tm, tn), lambda i, j: (i, j)),
        out_shape=jax.ShapeDtypeStruct((t, N_IN_COLS), BF16),
        scratch_shapes=[pltpu.VMEM((tm, D_MODEL), BF16)],
        compiler_params=_params(("arbitrary", "arbitrary")), name="inproj")(x, norm_g, sc, sh, w_bf)


def _norm_rope_all(xs, gains, c, s1, s2):
    ms = [jnp.mean(x * x, axis=-1, keepdims=True) for x in xs]
    ys = [x * lax.rsqrt(m + EPS) * g for x, m, g in zip(xs, ms, gains)]
    up = [pltpu.roll(y, ROT_DIMS // 2, 1) for y in ys]
    dn = [pltpu.roll(y, HEAD_DIM - ROT_DIMS // 2, 1) for y in ys]
    return [(y * c + u * s1 + d * s2).astype(BF16) for y, u, d in zip(ys, up, dn)]


def _rope_tables(pos):
    half = ROT_DIMS // 2
    inv = ROPE_THETA ** (-2.0 * jnp.arange(half, dtype=F32) / ROT_DIMS)
    ang = pos.astype(F32)[:, None] * inv[None, :]
    cos, sin = jnp.cos(ang), jnp.sin(ang)
    n = pos.shape[0]
    rest = HEAD_DIM - ROT_DIMS
    c = jnp.concatenate([cos, cos, jnp.ones((n, rest), F32)], axis=1)
    s1 = jnp.concatenate([jnp.zeros((n, half), F32), sin, jnp.zeros((n, rest), F32)], axis=1)
    s2 = jnp.concatenate([-sin, jnp.zeros((n, half + rest), F32)], axis=1)
    return c, s1, s2


def _lru_body(xa_ref, ya_ref, ga_ref, cs_ref, h0_ref, cw_ref, cb_ref, wra_ref, bra_ref, wri_ref, bri_ref,
              lam_ref, o_ref, nh_ref, nc_ref, xpad, a_scr, b_scr, g_scr, *, seq, rc, nsub):
    pad = SUBLANES
    xpad[0:pad, :] = jnp.zeros((pad, nsub * LRU_BS), F32)
    xpad[pad - (CONV_W - 1):pad, :] = cs_ref[...]
    xpad[pad:pad + seq, :] = xa_ref[...].astype(F32)
    nc_ref[...] = xpad[seq + pad - (CONV_W - 1):seq + pad, :]

    z = -lam_ref[...]
    softplus = jnp.maximum(z, 0.0) + jnp.log1p(jnp.exp(-jnp.abs(z)))
    clam_all = -LRU_C * softplus
    for sb in range(nsub):
        ls = slice(sb * LRU_BS, (sb + 1) * LRU_BS)
        clam = clam_all[:, ls]
        wra = wra_ref[sb].astype(BF16)
        wri = wri_ref[sb].astype(BF16)
        for c0 in range(0, seq, rc):
            slab = xpad[c0:c0 + pad + rc, ls]
            xc = cb_ref[:, ls]
            for tap in range(CONV_W):
                back = CONV_W - 1 - tap
                shifted = slab if back == 0 else pltpu.roll(slab, back, 0)
                xc = xc + shifted[pad:pad + rc, :] * cw_ref[tap:tap + 1, ls]
            xcb = xc.astype(BF16)
            r = jax.nn.sigmoid(jnp.dot(xcb, wra, preferred_element_type=F32) + bra_ref[sb])
            ig = jax.nn.sigmoid(jnp.dot(xcb, wri, preferred_element_type=F32) + bri_ref[sb])
            log_a = r * clam
            a_scr[c0:c0 + rc, ls] = jnp.exp(log_a)
            th = jnp.tanh(log_a)
            mult = jnp.sqrt(jnp.maximum(-2.0 * th / (1.0 - th), 0.0))
            b_scr[c0:c0 + rc, ls] = mult * (ig * xc)
            ya = ya_ref[c0:c0 + rc, ls].astype(F32)
            ga = ga_ref[c0:c0 + rc, ls].astype(F32)
            g_scr[c0:c0 + rc, ls] = jax.nn.sigmoid(ga) * jax.nn.gelu(ya)

    rows = 2 * SUBLANES
    rid = lax.broadcasted_iota(I32, (SUBLANES, LRU_BS), 0)

    def step(it, carry):
        r0 = pl.multiple_of(it * rows, rows)
        out = []
        for sb in range(nsub):
            ls = slice(sb * LRU_BS, (sb + 1) * LRU_BS)
            c = carry[:, ls]
            hs = []
            for half in range(rows // SUBLANES):
                rh = pl.multiple_of(r0 + half * SUBLANES, SUBLANES)
                a = a_scr[pl.ds(rh, SUBLANES), ls]
                b = b_scr[pl.ds(rh, SUBLANES), ls]
                for d in (1, 2, 4):
                    keep = rid >= d
                    a_s = jnp.where(keep, pltpu.roll(a, d, 0), 1.0)
                    b_s = jnp.where(keep, pltpu.roll(b, d, 0), 0.0)
                    b = a * b_s + b
                    a = a * a_s
                hs.append(b + a * c)
                c = a[SUBLANES - 1:SUBLANES, :] * c + b[SUBLANES - 1:SUBLANES, :]
            out.append(c)
            h = jnp.concatenate(hs, axis=0)
            o_ref[pl.ds(r0, rows), ls] = (g_scr[pl.ds(r0, rows), ls] * h).astype(o_ref.dtype)
        return jnp.concatenate(out, axis=1) if nsub > 1 else out[0]

    nh_ref[...] = lax.fori_loop(0, seq // rows, step, h0_ref[...], unroll=min(2, seq // rows))


def _lru(proj, conv_state, h0, conv_w, conv_b, w_ra, b_ra, w_ri, b_ri, lam, batch, seq, nsub):
    t = proj.shape[0]
    rc = min(seq, 256)
    cw = nsub * LRU_BS
    col = lambda off: (lambda b, c: (b, off // cw + c))
    vec = lambda: pl.BlockSpec((1, cw), lambda b, c: (0, c))
    blk3 = lambda r: pl.BlockSpec((None, r, cw), lambda b, c: (b, 0, c))
    wblk = lambda: pl.BlockSpec((nsub, LRU_BS, LRU_BS), lambda b, c: (c, 0, 0))
    bblk = lambda: pl.BlockSpec((nsub, 1, LRU_BS), lambda b, c: (c, 0, 0))
    return pl.pallas_call(
        functools.partial(_lru_body, seq=seq, rc=rc, nsub=nsub), grid=(batch, LRU_BLOCKS // nsub),
        in_specs=[pl.BlockSpec((seq, cw), col(COL_XA)), pl.BlockSpec((seq, cw), col(COL_YA)),
                  pl.BlockSpec((seq, cw), col(COL_GA)), blk3(CONV_W - 1), blk3(1),
                  pl.BlockSpec((CONV_W, cw), lambda b, c: (0, c)), vec(), wblk(), bblk(), wblk(), bblk(), vec()],
        out_specs=[pl.BlockSpec((seq, cw), lambda b, c: (b, c)), blk3(1), blk3(CONV_W - 1)],
        out_shape=[jax.ShapeDtypeStruct((t, D_MODEL), BF16), jax.ShapeDtypeStruct((batch, 1, D_MODEL), F32),
                   jax.ShapeDtypeStruct((batch, CONV_W - 1, D_MODEL), F32)],
        scratch_shapes=[pltpu.VMEM((seq + SUBLANES, cw), F32), pltpu.VMEM((seq, cw), F32),
                        pltpu.VMEM((seq, cw), F32), pltpu.VMEM((seq, cw), F32)],
        compiler_params=_params(("arbitrary", "arbitrary")), name="lru")(
            proj, proj, proj, conv_state, h0.reshape(batch, 1, D_MODEL), conv_w, conv_b.reshape(1, D_MODEL),
            w_ra, b_ra.reshape(LRU_BLOCKS, 1, LRU_BS), w_ri, b_ri.reshape(LRU_BLOCKS, 1, LRU_BS),
            lam.reshape(1, D_MODEL))


def _attn_body(*refs, banded, qt, npv, subs):
    if banded:
        (sink_ref, q_ref, kc_ref, vp_ref, vc_ref, gb_ref, ma_ref, qg_ref, kg_ref, c_ref, s1_ref, s2_ref,
         o_ref, kn_ref, kprev) = refs
    else:
        (sink_ref, q_ref, kp_ref, kc_ref, vp_ref, vc_ref, gb_ref, ma_ref, qg_ref, kg_ref, c_ref, s1_ref, s2_ref,
         o_ref, kn_ref) = refs
    nt = (((1,), (1,)), ((), ()))
    i = pl.program_id(1)
    log2e = math.log2(math.e)
    if banded:
        nk = npv + qt
        qc = lax.broadcasted_iota(I32, (qt, nk), 0) // CHUNK
        col = lax.broadcasted_iota(I32, (qt, nk), 1)
        band = ((col < npv) & (col // CHUNK >= qc)) | ((col >= npv) & ((col - npv) // CHUNK <= qc))

        @pl.when(i == 0)
        def _():
            kprev[...] = jnp.zeros_like(kprev)

    def logits2(q, k, ok):
        s = lax.dot_general(q, k, nt, preferred_element_type=F32) * (ATT_SCALE * log2e)
        return s if ok is None else jnp.where(ok, s, NEG)

    prev_k = None
    for sub in range(subs):
        rs = slice(sub * qt, (sub + 1) * qt)
        tab = (c_ref[rs, :], s1_ref[rs, :], s2_ref[rs, :])
        slabs = ([q_ref[rs, h * HEAD_DIM:(h + 1) * HEAD_DIM].astype(F32) for h in range(N_HEADS)]
                 + [kc_ref[rs, g * HEAD_DIM:(g + 1) * HEAD_DIM].astype(F32) for g in range(N_KV)])
        normed = _norm_rope_all(slabs, [qg_ref[...]] * N_HEADS + [kg_ref[...]] * N_KV, *tab)
        if banded:
            valid = band if sub > 0 else band & ((col >= npv) | (i > 0))
        units = []
        for g in range(N_KV):
            gs = slice(g * HEAD_DIM, (g + 1) * HEAD_DIM)
            kc = normed[N_HEADS + g]
            vc = vc_ref[rs, gs].astype(BF16)
            if sub > 0:
                kp, vp = prev_k[g], vc_ref[(sub - 1) * qt:sub * qt, gs].astype(BF16)
            elif banded:
                kp, vp = kprev[:, gs], vp_ref[:, gs].astype(BF16)
            else:
                kp, vp = kp_ref[:, gs].astype(BF16), vp_ref[:, gs].astype(BF16)
            if sub == subs - 1:
                kn_ref[:, gs] = kc
                if banded:
                    kprev[:, gs] = kc
            if banded:
                parts = [(jnp.concatenate([kp, kc], axis=0), jnp.concatenate([vp, vc], axis=0), valid)]
            else:
                parts = [(kp, vp, None), (kc, vc, None)]
            parts = [(k, jnp.concatenate([v, jnp.ones(v.shape, BF16)], axis=1), ok) for k, v, ok in parts]
            for h in range(g * GROUP, (g + 1) * GROUP):
                units.append(dict(h=h, q=normed[h], sk=jnp.full((qt, 1), sink_ref[h] * log2e, F32), parts=parts))
        prev_k = normed[N_HEADS:]

        for wave in range(0, N_HEADS, ATTN_WAVE):
            active = units[wave:wave + ATTN_WAVE]
            for u in active:
                u["s"] = [logits2(u["q"], k, ok) for k, _, ok in u["parts"]]
            for u in active:
                m = u["sk"]
                for s in u["s"]:
                    m = jnp.maximum(m, jnp.max(s, -1, keepdims=True))
                u["m"] = m
            for u in active:
                u["p"] = [jnp.exp2(s - u["m"]).astype(BF16) for s in u["s"]]
            for u in active:
                acc = jnp.dot(u["p"][0], u["parts"][0][1], preferred_element_type=F32)
                for p, (_, v, _) in zip(u["p"][1:], u["parts"][1:]):
                    acc = acc + jnp.dot(p, v, preferred_element_type=F32)
                den = acc[:, HEAD_DIM:] + jnp.exp2(u["sk"] - u["m"])
                u["o"] = acc[:, :HEAD_DIM] / den
            for u in active:
                hs = slice(u["h"] * HEAD_DIM, (u["h"] + 1) * HEAD_DIM)
                gate = jax.nn.sigmoid(gb_ref[rs, hs].astype(F32))
                o_ref[rs, hs] = (ma_ref[rs, hs].astype(F32) + gate * u["o"]).astype(o_ref.dtype)


def _attn(sink, proj, ma, tabs, q_norm_g, k_norm_g, k_cache2, v_cache2, batch, seq, qt, subs):
    t = proj.shape[0]
    nq = seq // qt
    ns = nq // subs
    banded = k_cache2 is None
    row = lambda width, col: pl.BlockSpec((subs * qt, width), lambda b, i: (b * ns + i, col))
    vec = lambda: pl.BlockSpec((1, HEAD_DIM), lambda b, i: (0, 0))
    tab = lambda: pl.BlockSpec((subs * qt, HEAD_DIM), lambda b, i: (i, 0))
    if banded:
        npv = qt
        prev_specs = [pl.BlockSpec((npv, KV_W), lambda b, i: (b * nq + jnp.maximum(i * subs - 1, 0), COL_V // KV_W))]
        prev_args = [proj]
        scratch = [pltpu.VMEM((qt, KV_W), BF16)]
    else:
        npv = k_cache2.shape[0] // batch
        prev_specs = [pl.BlockSpec((npv, KV_W), lambda b, i: (b, 0))] * 2
        prev_args = [k_cache2, v_cache2]
        scratch = []
    in_specs = ([pl.BlockSpec(memory_space=pltpu.SMEM), row(Q_W, COL_Q // Q_W)]
                + prev_specs[:-1] + [row(KV_W, COL_K // KV_W), prev_specs[-1], row(KV_W, COL_V // KV_W),
                                     row(Q_W, COL_GB // Q_W), row(Q_W, 0), vec(), vec(), tab(), tab(), tab()])
    args = ([sink, proj] + prev_args[:-1] + [proj, prev_args[-1], proj, proj, ma,
                                             q_norm_g.reshape(1, HEAD_DIM), k_norm_g.reshape(1, HEAD_DIM), *tabs])
    return pl.pallas_call(
        functools.partial(_attn_body, banded=banded, qt=qt, npv=npv, subs=subs), grid=(batch, ns),
        in_specs=in_specs,
        out_specs=[pl.BlockSpec((subs * qt, Q_W), lambda b, i: (b * ns + i, 0)),
                   pl.BlockSpec((None, qt, KV_W), lambda b, i: (b, 0, 0))],
        out_shape=[jax.ShapeDtypeStruct((t, Q_W), BF16), jax.ShapeDtypeStruct((batch, qt, KV_W), BF16)],
        scratch_shapes=scratch,
        compiler_params=_params(("arbitrary", "arbitrary")), name="attn")(*args)


def _lane_min(x):
    return jnp.min(x, axis=-1, keepdims=True)


OUTPROJ_CHUNKS = 4


def _outproj_body(x_ref, m_ref, w_ref, g1_ref, n2_ref, sc_ref, sh_ref, wrc_ref, wrh_ref, br_ref, cin_ref,
                  x1_ref, h2_ref, mf_ref, mi_ref, cnt_ref, base_scr, y_even, y_odd, *, tm):
    i = pl.program_id(0)

    @pl.when(i == 0)
    def _():
        base_scr[...] = cin_ref[...]
        y_odd[...] = jnp.zeros_like(y_odd)

    cw = D_MODEL // OUTPROJ_CHUNKS
    for parity, (y_new, y_old) in enumerate(((y_even, y_odd), (y_odd, y_even))):
        @pl.when(i % 2 == parity)
        def _(y_new=y_new, y_old=y_old):
            def product(c):
                y_new[:, c * cw:(c + 1) * cw] = jnp.dot(m_ref[...], w_ref[:, c * cw:(c + 1) * cw],
                                                        preferred_element_type=F32)

            _outproj_finish(y_old, i > 0, product, x_ref, g1_ref, n2_ref, sc_ref, sh_ref, wrc_ref, wrh_ref, br_ref,
                            x1_ref, h2_ref, mf_ref, mi_ref, cnt_ref, base_scr, tm)


def _outproj_finish(y_ref, live, product, x_ref, g1_ref, n2_ref, sc_ref, sh_ref, wrc_ref, wrh_ref, br_ref,
                    x1_ref, h2_ref, mf_ref, mi_ref, cnt_ref, base_scr, tm):
    product(0)
    x1 = x_ref[...] + g1_ref[...] * y_ref[...]
    x1_ref[...] = x1
    h2 = _norm_mod(x1, n2_ref[...], sc_ref[...], sh_ref[...])
    h2_ref[...] = h2
    product(1)

    hi = h2.astype(BF16)
    lo = (h2 - hi.astype(F32)).astype(BF16)
    both = jnp.dot(hi, wrc_ref[...], preferred_element_type=F32)
    logits = (both[:, :LANES] + jnp.dot(lo, wrh_ref[...], preferred_element_type=F32) + both[:, LANES:]) + br_ref[...]
    lane = lax.broadcasted_iota(I32, (tm, LANES), 1).astype(F32)
    ninf = -jnp.inf
    big = float(LANES)
    is_g = lane < N_EXP_GROUPS
    lg = jnp.where(is_g, logits, ninf)
    mg = jnp.max(lg, -1, keepdims=True)
    gsel = _lane_min(jnp.where(lg == mg, lane, big))
    p_g = 1.0 / jnp.sum(jnp.where(is_g, jnp.exp(lg - mg), 0.0), -1, keepdims=True)
    lo_lane = N_EXP_GROUPS + EXP_PER_GROUP * gsel
    le = jnp.where((lane >= lo_lane) & (lane < lo_lane + EXP_PER_GROUP), logits, ninf)
    m1 = jnp.max(le, -1, keepdims=True)
    i1 = _lane_min(jnp.where(le == m1, lane, big))
    le2 = jnp.where(lane == i1, ninf, le)
    m2 = jnp.max(le2, -1, keepdims=True)
    i2 = _lane_min(jnp.where(le2 == m2, lane, big))
    e21 = jnp.exp(m2 - m1)
    w1 = p_g / (1.0 + e21)
    w2 = p_g * e21 / (1.0 + e21)
    e1, e2 = i1 - N_EXP_GROUPS, i2 - N_EXP_GROUPS
    product(2)

    onehot = ((lane == e1) | (lane == e2))
    row = lax.broadcasted_iota(I32, (tm, tm), 0)
    colm = lax.broadcasted_iota(I32, (tm, tm), 1)
    lower = jnp.where(row > colm, 1.0, 0.0).astype(BF16)
    before = jnp.dot(lower, jnp.where(onehot, 1.0, 0.0).astype(BF16), preferred_element_type=F32) + base_scr[...]
    r1 = jnp.sum(jnp.where(lane == e1, before, 0.0), -1, keepdims=True)
    r2 = jnp.sum(jnp.where(lane == e2, before, 0.0), -1, keepdims=True)
    base_scr[...] = base_scr[...] + jnp.sum(jnp.where(onehot & live, 1.0, 0.0), axis=0, keepdims=True)
    cnt_ref[...] = base_scr[...]

    mf_ref[...] = jnp.where(lane == 0.0, w1, jnp.where(lane == 1.0, w2, 0.0))
    mi = jnp.where(lane == 0.0, e1, jnp.where(lane == 1.0, e2, jnp.where(lane == 2.0, r1, jnp.where(lane == 3.0, r2, 0.0))))
    mi_ref[...] = mi.astype(I32)
    product(3)


def _outproj(x, merged, w_bf, g1, norm2_g, sc, sh, wr_cat, wr_hi, br, cnt_in, per_token, tm, seq):
    t = x.shape[0]
    tpb = max(seq // tm, 1)
    n = t // tm
    done = lambda i: jnp.maximum(i - 1, 0)
    row = lambda: pl.BlockSpec((tm, D_MODEL), lambda i: (done(i), 0))
    full = lambda shape: pl.BlockSpec(shape, lambda i: (0, 0))
    meta = lambda: pl.BlockSpec((tm, LANES), lambda i: (done(i), 0))
    mod = lambda: _mod_spec(per_token, tm, tpb, done)
    return pl.pallas_call(
        functools.partial(_outproj_body, tm=tm), grid=(n + 1,),
        in_specs=[row(), pl.BlockSpec((tm, D_MODEL), lambda i: (jnp.minimum(i, n - 1), 0)),
                  pl.BlockSpec((D_MODEL, D_MODEL), lambda i: (0, 0), pipeline_mode=pl.Buffered(1)),
                  mod(), full((1, D_MODEL)), mod(), mod(),
                  full((D_MODEL, 2 * LANES)), full((D_MODEL, LANES)), full((1, LANES)), full((1, LANES))],
        out_specs=[row(), row(), meta(), meta(), full((1, LANES))],
        out_shape=[jax.ShapeDtypeStruct((t, D_MODEL), F32), jax.ShapeDtypeStruct((t, D_MODEL), F32),
                   jax.ShapeDtypeStruct((t, LANES), F32), jax.ShapeDtypeStruct((t, LANES), I32),
                   jax.ShapeDtypeStruct((1, LANES), F32)],
        scratch_shapes=[pltpu.VMEM((1, LANES), F32), pltpu.VMEM((tm, D_MODEL), F32), pltpu.VMEM((tm, D_MODEL), F32)],
        compiler_params=_params(("arbitrary",)), name="outproj")(
            x, merged, w_bf, g1, norm2_g, sc, sh, wr_cat, wr_hi, br, cnt_in)


def _lane_cumsum(x):
    lane = lax.broadcasted_iota(I32, x.shape, 1)
    for d in (1, 2, 4, 8, 16):
        x = x + jnp.where(lane >= d, pltpu.roll(x, d, 1), 0.0)
    return x


def _plan_body(cnt_ref, mi_ref, dest_ref, bexp_ref, pend_ref, *, tp, blk, nb_pad):
    cnt = jnp.broadcast_to(cnt_ref[...], (SUBLANES, LANES))
    lane8 = lax.broadcasted_iota(I32, (SUBLANES, LANES), 1)
    padded = jnp.where(lane8 < N_EXPERTS, jnp.ceil(cnt / blk) * blk, 0.0)
    pend = _lane_cumsum(padded)
    pstart = (pend - padded)[0:1, :]
    mi = mi_ref[...]
    lane = lax.broadcasted_iota(I32, (tp, LANES), 1)
    e1, e2, r1, r2 = mi[:, 0:1], mi[:, 1:2], mi[:, 2:3], mi[:, 3:4]
    d1 = jnp.sum(jnp.where(lane == e1, pstart, 0.0), -1, keepdims=True).astype(I32) + r1
    d2 = jnp.sum(jnp.where(lane == e2, pstart, 0.0), -1, keepdims=True).astype(I32) + r2
    dest_ref[...] = jnp.where(lane == 0, d1, jnp.where(lane == 1, d2, 0))

    @pl.when(pl.program_id(0) == 0)
    def _():
        jb = lax.broadcasted_iota(I32, (nb_pad, LANES), 0).astype(F32) * blk
        ln = lax.broadcasted_iota(I32, (nb_pad, LANES), 1)
        ends = jnp.where((ln < N_EXPERTS) & (pend[0:1, :] <= jb), 1.0, 0.0)
        be = jnp.minimum(jnp.sum(ends, -1, keepdims=True), N_EXPERTS - 1.0)
        used = pend[0:1, N_EXPERTS - 1:N_EXPERTS] / blk
        bexp_ref[...] = jnp.where(ln == 0, be, jnp.where(ln == 1, used, 0.0)).astype(I32)
        pend_ref[...] = pend.astype(I32)


def _plan(cnt, mi, blk, nb_pad):
    t = mi.shape[0]
    tp = min(t, 4096)
    return pl.pallas_call(
        functools.partial(_plan_body, tp=tp, blk=float(blk), nb_pad=nb_pad), grid=(t // tp,),
        in_specs=[pl.BlockSpec((1, LANES), lambda i: (0, 0)), pl.BlockSpec((tp, LANES), lambda i: (i, 0))],
        out_specs=[pl.BlockSpec((tp, LANES), lambda i: (i, 0)), pl.BlockSpec((nb_pad, LANES), lambda i: (0, 0)),
                   pl.BlockSpec((SUBLANES, LANES), lambda i: (0, 0))],
        out_shape=[jax.ShapeDtypeStruct((t, LANES), I32), jax.ShapeDtypeStruct((nb_pad, LANES), I32),
                   jax.ShapeDtypeStruct((SUBLANES, LANES), I32)],
        compiler_params=_params(("arbitrary",)), name="plan")(cnt, mi)


DISPATCH_SLOTS = 4


def _dispatch_body(*refs, tm, blk, bounds):
    pend_ref, dest_ref = refs[:2]
    h2_refs = refs[2:2 + len(bounds)]
    xs_ref, zbuf, tiles, lsem, rsem, zsem = refs[2 + len(bounds):]
    nsteps = bounds[-1][1]
    i = pl.program_id(0)

    def zero_copy(e):
        return pltpu.make_async_copy(zbuf, xs_ref.at[pl.ds(pl.multiple_of(pend_ref[e] - blk, blk), blk)], zsem)

    def has_rows(e):
        return pend_ref[e] > jnp.where(e > 0, pend_ref[jnp.maximum(e - 1, 0)], 0)

    @pl.when(i == 0)
    def _():
        zbuf[...] = jnp.zeros_like(zbuf)

        def zstart(e, c):
            @pl.when(has_rows(e))
            def _():
                zero_copy(e).start()
            return c

        def zwait(e, c):
            @pl.when(has_rows(e))
            def _():
                zero_copy(e).wait()
            return c

        lax.fori_loop(0, N_EXPERTS, zstart, 0)
        lax.fori_loop(0, N_EXPERTS, zwait, 0)

        def tail_copy(j):
            return pltpu.make_async_copy(zbuf, xs_ref.at[pl.ds(pl.multiple_of(j * blk, blk), blk)], zsem)

        def tstart(j, c):
            tail_copy(j).start()
            return c

        def twait(j, c):
            tail_copy(j).wait()
            return c

        used = pend_ref[N_EXPERTS - 1] // blk
        lax.fori_loop(used, xs_ref.shape[0] // blk, tstart, 0)
        lax.fori_loop(used, xs_ref.shape[0] // blk, twait, 0)

    def load_start(j):
        s = j % DISPATCH_SLOTS
        for (lo, hi), h2_ref in zip(bounds, h2_refs):
            @pl.when((j >= lo) & (j < hi))
            def _(lo=lo, h2_ref=h2_ref):
                pltpu.make_async_copy(h2_ref.at[pl.ds(pl.multiple_of((j - lo) * tm, tm), tm)], tiles.at[s],
                                      lsem.at[s]).start()

    def load_wait(j):
        s = j % DISPATCH_SLOTS
        pltpu.make_async_copy(h2_refs[0].at[pl.ds(0, tm)], tiles.at[s], lsem.at[s]).wait()

    def wait_rows(j):
        s = j % DISPATCH_SLOTS
        for _ in range(2):
            pltpu.make_async_copy(tiles.at[s], xs_ref.at[pl.ds(0, tm)], rsem.at[s]).wait()

    @pl.when(i == 0)
    def _():
        load_start(0)
        if nsteps > 1:
            load_start(1)

    @pl.when(i >= 2)
    def _():
        wait_rows(i - 2)

    @pl.when(i + 2 < nsteps)
    def _():
        load_start(i + 2)

    load_wait(i)
    slot = i % DISPATCH_SLOTS

    def start(r, c):
        for k in range(2):
            pltpu.make_async_copy(tiles.at[slot, pl.ds(r, 1)], xs_ref.at[pl.ds(dest_ref[0, k * tm + r], 1)],
                                  rsem.at[slot]).start(priority=k)
        return c

    lax.fori_loop(0, tm, start, 0, unroll=8)

    @pl.when(i == nsteps - 1)
    def _():
        if nsteps > 1:
            wait_rows(i - 1)
        wait_rows(i)


def _dispatch(pend, dest3, h2_list, n_rows, tm, blk):
    bounds, lo = [], 0
    for h2 in h2_list:
        bounds.append((lo, lo + h2.shape[0] // tm))
        lo = bounds[-1][1]
    return pl.pallas_call(
        functools.partial(_dispatch_body, tm=tm, blk=blk, bounds=tuple(bounds)), grid=(lo,),
        in_specs=[pl.BlockSpec(memory_space=pltpu.SMEM),
                  pl.BlockSpec((None, 1, 2 * tm), lambda i: (i, 0, 0), memory_space=pltpu.SMEM)]
                 + [pl.BlockSpec(memory_space=pl.ANY)] * len(h2_list),
        out_specs=pl.BlockSpec(memory_space=pl.ANY),
        out_shape=jax.ShapeDtypeStruct((n_rows, D_MODEL), F32),
        scratch_shapes=[pltpu.VMEM((blk, D_MODEL), F32), pltpu.VMEM((DISPATCH_SLOTS, tm, D_MODEL), F32),
                        pltpu.SemaphoreType.DMA((DISPATCH_SLOTS,)), pltpu.SemaphoreType.DMA((DISPATCH_SLOTS,)),
                        pltpu.SemaphoreType.DMA(())],
        compiler_params=_params(("arbitrary",)), name="dispatch")(pend, dest3, *h2_list)


MOE_ROW_SLOTS = 3


def _moe_body(bexp_ref, used_ref, xs_hbm, wu_hbm, wd_hbm, ys_ref, xbuf, wu_f, wd_f, wu_bf, wd_bf, slot_ref,
              sem, xsem, *, blk):
    j = pl.program_id(0)
    used = used_ref[0]

    def rows(jj):
        s = jj % MOE_ROW_SLOTS
        return pltpu.make_async_copy(xs_hbm.at[pl.ds(pl.multiple_of(jj * blk, blk), blk)], xbuf.at[s], xsem.at[s])

    for ahead in range(MOE_ROW_SLOTS - 1):
        @pl.when((j == 0) & (ahead < used))
        def _(ahead=ahead):
            rows(ahead).start()

    @pl.when(j + MOE_ROW_SLOTS - 1 < used)
    def _():
        rows(j + MOE_ROW_SLOTS - 1).start()
    e = bexp_ref[j]
    live = j < used
    changed = (j == 0) | (e != bexp_ref[jnp.maximum(j - 1, 0)])

    def fetch(expert, slot):
        return (pltpu.make_async_copy(wu_hbm.at[expert], wu_f.at[slot], sem.at[slot]),
                pltpu.make_async_copy(wd_hbm.at[expert], wd_f.at[slot], sem.at[slot]))

    @pl.when((j == 0) & live)
    def _():
        slot_ref[0] = 0
        for copy in fetch(e, 0):
            copy.start()

    @pl.when(changed & live)
    def _():
        slot = slot_ref[0]
        for copy in fetch(e, slot):
            copy.wait()
        last = pl.num_programs(0) - 1
        nxt = lax.while_loop(lambda c: (c < used) & (bexp_ref[jnp.minimum(c, last)] == e), lambda c: c + 1, j + 1)

        @pl.when(nxt < used)
        def _():
            for copy in fetch(bexp_ref[jnp.minimum(nxt, last)], 1 - slot):
                copy.start()

        wu_bf[...] = wu_f[slot].astype(BF16)
        wd_bf[...] = wd_f[slot].astype(BF16)
        slot_ref[0] = 1 - slot

    @pl.when(live)
    def _():
        rows(j).wait()
        up = jnp.dot(xbuf[j % MOE_ROW_SLOTS].astype(BF16), wu_bf[...], preferred_element_type=F32)
        act = jax.nn.silu(up[:, :D_EXPERT]) * up[:, D_EXPERT:]
        ys_ref[...] = jnp.dot(act.astype(BF16), wd_bf[...], preferred_element_type=F32)

    @pl.when(j >= used)
    def _():
        ys_ref[...] = jnp.zeros_like(ys_ref)


def _moe(bexp, used, xs, w_up, w_down, blk):
    p = xs.shape[0]
    grid_spec = pltpu.PrefetchScalarGridSpec(
        num_scalar_prefetch=2, grid=(p // blk,),
        in_specs=[pl.BlockSpec(memory_space=pl.ANY), pl.BlockSpec(memory_space=pl.ANY),
                  pl.BlockSpec(memory_space=pl.ANY)],
        out_specs=pl.BlockSpec((blk, D_MODEL), lambda j, be, us: (j, 0)),
        scratch_shapes=[pltpu.VMEM((MOE_ROW_SLOTS, blk, D_MODEL), F32),
                        pltpu.VMEM((2, D_MODEL, 2 * D_EXPERT), F32), pltpu.VMEM((2, D_EXPERT, D_MODEL), F32),
                        pltpu.VMEM((D_MODEL, 2 * D_EXPERT), BF16), pltpu.VMEM((D_EXPERT, D_MODEL), BF16),
                        pltpu.SMEM((1,), I32), pltpu.SemaphoreType.DMA((2,)),
                        pltpu.SemaphoreType.DMA((MOE_ROW_SLOTS,))])
    return pl.pallas_call(
        functools.partial(_moe_body, blk=blk), grid_spec=grid_spec, out_shape=jax.ShapeDtypeStruct((p, D_MODEL), F32),
        compiler_params=_params(("arbitrary",)), name="moe")(bexp, used, xs, w_up, w_down)


def _combine_body(dest_ref, dnext_ref, x1_ref, mf_ref, g2_ref, ys_ref, o_ref, ybuf, sem, *, tm, nsteps):
    i = pl.program_id(0)

    def issue(dref, s):
        def start(r, c):
            for k in range(2):
                pltpu.make_async_copy(ys_ref.at[pl.ds(dref[0, k * tm + r], 1)], ybuf.at[s, k, pl.ds(r, 1)],
                                      sem.at[s]).start(priority=k)
            return c

        lax.fori_loop(0, tm, start, 0, unroll=8)

    @pl.when(i == 0)
    def _():
        issue(dest_ref, 0)

    for s in range(2):
        @pl.when(i % 2 == s)
        def _(s=s):
            @pl.when(i + 1 < nsteps)
            def _():
                issue(dnext_ref, 1 - s)

            for k in range(2):
                pltpu.make_async_copy(ys_ref.at[pl.ds(0, tm)], ybuf.at[s, k], sem.at[s]).wait()
            mf = mf_ref[...]
            moe = mf[:, 0:1] * ybuf[s, 0] + mf[:, 1:2] * ybuf[s, 1]
            o_ref[...] = x1_ref[...] + g2_ref[...] * moe


def _combine(dest3, x1, mf, g2, ys, per_token, tm, seq):
    t = x1.shape[0]
    tpb = max(seq // tm, 1)
    nsteps = t // tm
    return pl.pallas_call(
        functools.partial(_combine_body, tm=tm, nsteps=nsteps), grid=(nsteps,),
        in_specs=[pl.BlockSpec((None, 1, 2 * tm), lambda i: (i, 0, 0), memory_space=pltpu.SMEM),
                  pl.BlockSpec((None, 1, 2 * tm), lambda i: (jnp.minimum(i + 1, nsteps - 1), 0, 0),
                               memory_space=pltpu.SMEM),
                  pl.BlockSpec((tm, D_MODEL), lambda i: (i, 0)),
                  pl.BlockSpec((tm, LANES), lambda i: (i, 0)),
                  _mod_spec(per_token, tm, tpb),
                  pl.BlockSpec(memory_space=pl.ANY)],
        out_specs=pl.BlockSpec((tm, D_MODEL), lambda i: (i, 0)),
        out_shape=jax.ShapeDtypeStruct((t, D_MODEL), F32),
        scratch_shapes=[pltpu.VMEM((2, 2, tm, D_MODEL), F32), pltpu.SemaphoreType.DMA((2,))],
        compiler_params=_params(("arbitrary",)), name="combine")(dest3, dest3, x1, mf, g2, ys)


def _group_mod(mod6, per_token, seq):
    if per_token:
        return jnp.repeat(mod6, seq, axis=0)
    return mod6[:, None, :]


def _front(x, mods, pos, conv_state, h0, k_cache, v_cache, w, cnt_in, seq, tm_in, tm_out, nsub):
    batch = x.shape[0]
    t = batch * seq
    xt = x.reshape(t, D_MODEL)
    per_token = seq < tm_out
    sh1, sc1, g1, sh2, sc2, g2 = [_group_mod(m, per_token, seq) for m in mods]

    proj = _inproj(xt, w["norm1_g"], sc1, sh1, w["w_in"], per_token, tm_in, seq)

    ma, new_h, new_conv = _lru(proj, conv_state, h0, w["conv_w"], w["conv_b"], w["w_ra"], w["b_ra"],
                               w["w_ri"], w["b_ri"], w["lru_lambda"], batch, seq, nsub)

    tabs = _rope_tables(pos)
    keep = min(WINDOW, seq)
    if k_cache is None:
        merged, kn = _attn(w["attn_sink"], proj, ma, tabs, w["q_norm_g"], w["k_norm_g"], None, None, batch, seq, keep,
                           subs=2 if seq % (2 * keep) == 0 else 1)
    else:
        rows = k_cache.shape[1]
        merged, kn = _attn(w["attn_sink"], proj, ma, tabs, w["q_norm_g"], w["k_norm_g"],
                           k_cache.reshape(batch * rows, KV_W), v_cache.reshape(batch * rows, KV_W), batch, seq, keep,
                           subs=1)
    v3 = proj.reshape(batch, seq, N_IN_COLS)[:, seq - keep:, COL_V:COL_V + KV_W]
    new_k = kn.astype(F32).reshape(batch, keep, N_KV, HEAD_DIM)
    new_v = v3.astype(F32).reshape(batch, keep, N_KV, HEAD_DIM)
    if k_cache is not None:
        new_k = jnp.concatenate([k_cache, new_k], axis=1)[:, -rows:]
        new_v = jnp.concatenate([v_cache, new_v], axis=1)[:, -rows:]

    x1, h2, mf, mi, cnt = _outproj(xt, merged, w["w_out"], g1, w["norm2_g"], sc2, sh2,
                                   w["wr_cat"], w["wr_hi"], w["br"], cnt_in, per_token, tm_out, seq)
    return dict(batch=batch, seq=seq, t=t, per_token=per_token, tm_out=tm_out, x1=x1, h2=h2, mf=mf, mi=mi, cnt=cnt,
                g2=g2, state=(new_h.reshape(batch, D_MODEL), new_conv, new_k, new_v))


def _flat_dest(dest, tm):
    t = dest.shape[0]
    return dest[:, 0:2].reshape(t // tm, tm, 2).transpose(0, 2, 1).reshape(t // tm, 1, 2 * tm)


def _experts(groups, w, blk):
    total = sum(gr["t"] for gr in groups)
    n_blocks = -(-(2 * total) // blk) + N_EXPERTS
    nb_pad = -(-n_blocks // SUBLANES) * SUBLANES
    cnt = groups[-1]["cnt"]
    tmd = min(min(gr["t"] for gr in groups), 256)
    dest_d = []
    for gr in groups:
        dest, bmeta, pend = _plan(cnt, gr["mi"], blk, nb_pad)
        gr["dest_c"] = _flat_dest(dest, gr["tm_out"])
        dest_d.append(_flat_dest(dest, tmd))
    xs = _dispatch(pend[0, :N_EXPERTS], jnp.concatenate(dest_d, axis=0), [gr["h2"] for gr in groups],
                   n_blocks * blk, tmd, blk)
    ys = _moe(bmeta[:n_blocks, 0], bmeta[0:1, 1], xs, w["w_exp_up"], w["w_exp_down"], blk)
    return [_combine(gr["dest_c"], gr["x1"], gr["mf"], gr["g2"], ys, gr["per_token"], gr["tm_out"], gr["seq"])
            .reshape(gr["batch"], gr["seq"], D_MODEL) for gr in groups]


def kernel(x_prompt, x_sample, state_lru_h, state_lru_conv, cache_swa_k, cache_swa_v, c_prompt, c_sample,
           norm1_g, norm2_g, w_ada, b_ada, w_in, conv_w, conv_b, w_ra, b_ra, w_ri, b_ri, lru_lambda,
           q_norm_g, k_norm_g, attn_sink, w_out, w_router_g, b_router_g, w_router_e, b_router_e,
           w_exp_up, w_exp_down):
    assert norm1_g.shape[0] == 1, "single-layer trunk"
    bp, sp, _ = x_prompt.shape
    bs, ss, _ = x_sample.shape

    mod = _ada(jnp.concatenate([c_prompt, c_sample], axis=0), w_ada[0], b_ada[0])
    mods_p = jnp.split(mod[:bp], 6, axis=-1)
    mods_s = jnp.split(mod[bp:], 6, axis=-1)

    pad = LANES - N_EXP_GROUPS - N_EXPERTS
    wr = jnp.concatenate([w_router_g[0], w_router_e[0], jnp.zeros((D_MODEL, pad), F32)], axis=1)
    wr_hi = wr.astype(BF16)
    wr_lo = (wr - wr_hi.astype(F32)).astype(BF16)
    br = jnp.concatenate([b_router_g[0], b_router_e[0], jnp.zeros((pad,), F32)]).reshape(1, LANES)

    w = dict(norm1_g=norm1_g[0].reshape(1, D_MODEL), norm2_g=norm2_g[0].reshape(1, D_MODEL),
             w_in=w_in[0].astype(BF16), conv_w=conv_w[0], conv_b=conv_b[0], w_ra=w_ra[0], b_ra=b_ra[0],
             w_ri=w_ri[0], b_ri=b_ri[0], lru_lambda=lru_lambda[0], q_norm_g=q_norm_g[0], k_norm_g=k_norm_g[0],
             attn_sink=attn_sink[0], w_out=w_out[0].astype(BF16), wr_hi=wr_hi,
             wr_cat=jnp.concatenate([wr_hi, wr_lo], axis=1), br=br,
             w_exp_up=w_exp_up[0], w_exp_down=w_exp_down[0])

    prompt = _front(x_prompt, mods_p, jnp.arange(sp), jnp.zeros((bp, CONV_W - 1, D_MODEL), F32),
                    jnp.zeros((bp, D_MODEL), F32), None, None, w, jnp.zeros((1, LANES), F32), sp,
                    tm_in=min(1024, sp), tm_out=min(512, sp), nsub=4)
    sample = _front(x_sample, mods_s, PAST_LEN + jnp.arange(ss), state_lru_conv[0], state_lru_h[0],
                    cache_swa_k[0], cache_swa_v[0], w, prompt["cnt"], ss,
                    tm_in=bs * ss, tm_out=bs * ss, nsub=LRU_BLOCKS)
    yp, ys = _experts([prompt, sample], w, blk=256)
    ph, pc, pk, pv = prompt["state"]
    sh, sc, sk, sv = sample["state"]
    return (yp, ys, ph[None], pc[None], pk[None], pv[None], sh[None], sc[None], sk[None], sv[None])
```
